```python
import jax, jax.numpy as jnp
from jax import lax
import numpy as np

D_MODEL = 1024
BATCH = 32
SEQ = 256
DEPTH = 4
DEC_BATCH = 8
DEC_SEQ = 1024
PAST_LEN = 512

GRID_W = 64
N_MIXERS = 4
CHUNK = 32
ALPHA = (2.0 * DEPTH) ** 0.25
BETA = (8.0 * DEPTH) ** -0.25
EPS = 1e-5

GLA_HEADS = 4
GLA_DK = D_MODEL // (2 * GLA_HEADS)
GLA_DV = D_MODEL // GLA_HEADS
GLA_RANK = 16
GLA_TAU = 16.0
ML_HEADS = 4
ML_D = D_MODEL // ML_HEADS
RET_HEADS = 4
RET_DK = D_MODEL // RET_HEADS
RET_DV = 2 * D_MODEL // RET_HEADS
ROPE_BASE = 10000.0
HG_HEADS = 8
HG_DF = 128
HG_DI = D_MODEL // HG_HEADS
D_FF = 2816
N_EXPERTS = 8
TOP_K = 2

N_GLA = (DEPTH + 3) // 4
N_MLSTM = (DEPTH + 2) // 4
N_RET = (DEPTH + 1) // 4
N_HGRN = DEPTH // 4
N_DENSE = (DEPTH + 1) // 2
N_MOE = DEPTH // 2

kernel_name = "bidir_hybrid_recurrent_dit_step"


def _heads(x, n_heads):
    b, t, _ = x.shape
    return jnp.transpose(x.reshape(b, t, n_heads, -1), (0, 2, 1, 3))


def _merge(x):
    b, h, t, d = x.shape
    return jnp.transpose(x, (0, 2, 1, 3)).reshape(b, t, h * d)


def _norm(x, w, center):
    xf = x.astype(jnp.float32)
    if center:
        xf = xf - jnp.mean(xf, axis=-1, keepdims=True)
    xf = xf * lax.rsqrt(jnp.mean(xf * xf, axis=-1, keepdims=True) + EPS)
    return xf.astype(x.dtype) * w


def _layer_norm(x, g, b):
    return _norm(x, g, True) + b


def _to_chunks(a):
    b, h, t = a.shape[:3]
    return jnp.moveaxis(a.astype(jnp.float32).reshape(b, h, t // CHUNK, CHUNK, *a.shape[3:]), 2, 0)


def _from_chunks(a):
    n, b, h, c, d = a.shape
    return jnp.moveaxis(a, 0, 2).reshape(b, h, n * c, d)


def _chunk_linear_scan(q, k, v, g, s0):
    causal = jnp.tril(jnp.ones((CHUNK, CHUNK), dtype=bool))
    per_dim = g.shape[-1] != 1

    def step(s, inp):
        qb, kb, vb, gb = inp
        bcum = jnp.cumsum(gb, axis=2)
        inter = jnp.einsum("bhcd,bhde->bhce", qb * jnp.exp(bcum), s)
        diff = bcum[:, :, :, None, :] - bcum[:, :, None, :, :]
        decay = jnp.exp(jnp.where(causal[:, :, None], diff, -jnp.inf))
        if per_dim:
            att = jnp.einsum("bhid,bhjd,bhijd->bhij", qb, kb, decay)
        else:
            att = jnp.einsum("bhid,bhjd->bhij", qb, kb) * decay[..., 0]
        o = inter + jnp.einsum("bhij,bhje->bhie", att, vb)
        b_last = bcum[:, :, -1:, :]
        s_new = jnp.exp(b_last[:, :, 0, :])[..., None] * s + jnp.einsum(
            "bhcd,bhce->bhde", kb * jnp.exp(b_last - bcum), vb)
        return s_new, o

    s_fin, o = lax.scan(step, s0.astype(jnp.float32), tuple(map(_to_chunks, (q, k, v, g))))
    return _from_chunks(o).astype(q.dtype), s_fin.astype(q.dtype)


def _bidir_linear(q, ks, v, gs, s0):
    flip = lambda a: jnp.flip(a, axis=2)
    o_f, s_f = _chunk_linear_scan(q, ks[0], v, gs[0], s0[:, 0])
    o_b, s_b = _chunk_linear_scan(flip(q), flip(ks[1]), flip(v), flip(gs[1]), s0[:, 1])
    return o_f + flip(o_b), jnp.stack([s_f, s_b], axis=1)


def _chunk_mlstm(q, k, v, i_pre, logf, c0, n0, m0):
    causal = jnp.tril(jnp.ones((CHUNK, CHUNK), dtype=bool))

    def step(carry, inp):
        c, nv, m = carry
        qb, kb, vb, ib, fb = inp
        bcum = jnp.cumsum(fb, axis=-1)
        logw = jnp.where(causal, bcum[..., :, None] - bcum[..., None, :] + ib[..., None, :], -jnp.inf)
        log_prev = bcum + m[..., None]
        m_t = jnp.maximum(log_prev, jnp.max(logw, axis=-1))
        w_prev = jnp.exp(log_prev - m_t)
        qk = jnp.einsum("bhid,bhjd->bhij", qb, kb) * jnp.exp(logw - m_t[..., None])
        num = w_prev[..., None] * jnp.einsum("bhcd,bhde->bhce", qb, c) + jnp.einsum("bhij,bhje->bhie", qk, vb)
        den = w_prev * jnp.einsum("bhcd,bhd->bhc", qb, nv) + jnp.sum(qk, axis=-1)
        out = num / jnp.maximum(jnp.abs(den), jnp.exp(-m_t))[..., None]
        m_new = m_t[..., -1]
        w_end = jnp.exp(bcum[..., -1:] - bcum + ib - m_new[..., None])
        dec = jnp.exp(bcum[..., -1] + m - m_new)
        c_new = dec[..., None, None] * c + jnp.einsum("bhcd,bhce->bhde", kb * w_end[..., None], vb)
        n_new = dec[..., None] * nv + jnp.einsum("bhcd,bhc->bhd", kb, w_end)
        return (c_new, n_new, m_new), out

    f32 = lambda a: a.astype(jnp.float32)
    (c_f, n_f, m_f), o = lax.scan(step, (f32(c0), f32(n0), f32(m0)),
                                  tuple(map(_to_chunks, (q, k, v, i_pre, logf))))
    return _from_chunks(o).astype(q.dtype), (c_f.astype(q.dtype), n_f.astype(q.dtype), m_f.astype(q.dtype))


def _rope_2d(x, rows, cols):
    d = x.shape[-1]
    quarter = d // 4
    inv = jnp.power(ROPE_BASE, -jnp.arange(quarter, dtype=jnp.float32) / quarter)
    ang = jnp.concatenate([rows[:, None] * inv, cols[:, None] * inv], axis=-1)
    cos, sin = jnp.cos(ang), jnp.sin(ang)
    xf = x.astype(jnp.float32)
    x1, x2 = xf[..., : d // 2], xf[..., d // 2:]
    return jnp.concatenate([x1 * cos - x2 * sin, x1 * sin + x2 * cos], axis=-1).astype(x.dtype)


def _gla_mixer(h, j, p, s0):
    nk, nv = GLA_HEADS * GLA_DK, GLA_HEADS * GLA_DV
    q, k, v, r = jnp.split(h @ p["gla_w_in"][j], [nk, 2 * nk, 2 * nk + nv], axis=-1)
    q = _heads(q, GLA_HEADS) * GLA_DK ** -0.5
    k, v = _heads(k, GLA_HEADS), _heads(v, GLA_HEADS)
    gates = []
    for d in range(2):
        z = (h @ p["gla_w_g1"][j, d]) @ p["gla_w_g2"][j, d] + p["gla_b_g"][j, d]
        gates.append(_heads(jax.nn.log_sigmoid(z.astype(jnp.float32)) / GLA_TAU, GLA_HEADS))
    o, s = _bidir_linear(q, (k, k), v, gates, s0)
    o = _merge(_norm(o, p["gla_norm_w"][j], False)) * jax.nn.silu(r)
    return o @ p["gla_w_out"][j], s


def _mlstm_mixer(h, j, p, c0, n0, m0):
    q, k, v, og = jnp.split(h @ p["ml_w_in"][j], 4, axis=-1)
    q, k, v = _heads(q, ML_HEADS), _heads(k, ML_HEADS) * ML_D ** -0.5, _heads(v, ML_HEADS)
    flip = lambda a: jnp.flip(a, axis=2)
    outs, cs, ns, ms = [], [], [], []
    for d in range(2):
        z = jnp.transpose((h @ p["ml_w_gate"][j, d] + p["ml_b_gate"][j, d]).astype(jnp.float32), (0, 2, 1))
        seq = (q, k, v, z[:, :ML_HEADS], jax.nn.log_sigmoid(z[:, ML_HEADS:]))
        if d == 1:
            seq = tuple(flip(a) for a in seq)
        hd, (cd, nd, md) = _chunk_mlstm(*seq, c0[:, d], n0[:, d], m0[:, d])
        outs.append(hd if d == 0 else flip(hd))
        cs.append(cd)
        ns.append(nd)
        ms.append(md)
    hsum = outs[0] + outs[1]
    y = _merge(_norm(hsum, p["ml_norm_w"][j].reshape(ML_HEADS, 1, ML_D), True)) * jax.nn.sigmoid(og)
    return y @ p["ml_w_out"][j], (jnp.stack(cs, axis=1), jnp.stack(ns, axis=1), jnp.stack(ms, axis=1))


def _ret_mixer(h, j, p, s0, pos):
    nk, nv = RET_HEADS * RET_DK, RET_HEADS * RET_DV
    q, k, v, g = jnp.split(h @ p["ret_w_in"][j], [nk, 2 * nk, 2 * nk + nv], axis=-1)
    q, k, v = _heads(q, RET_HEADS), _heads(k, RET_HEADS) * RET_DK ** -0.5, _heads(v, RET_HEADS)
    if pos is not None:
        q, k = _rope_2d(q, *pos), _rope_2d(k, *pos)
    b, _, t, _ = q.shape
    log_gamma = jax.nn.log_sigmoid(p["ret_decay"][j].astype(jnp.float32))
    gs = [jnp.broadcast_to(log_gamma[d][None, :, None, None], (b, RET_HEADS, t, 1)) for d in range(2)]
    o, s = _bidir_linear(q, (k, k), v, gs, s0)
    o = _merge(_norm(o, p["ret_norm_w"][j].reshape(RET_HEADS, 1, RET_DV), True)) * jax.nn.silu(g)
    return o @ p["ret_w_out"][j], s


def _hgrn_mixer(h, j, layer, p, s0):
    nf = HG_HEADS * HG_DF
    q, i_in, g = jnp.split(h @ p["hg_w_in"][j], [nf, nf + HG_HEADS * HG_DI], axis=-1)
    q, v = _heads(jax.nn.silu(q), HG_HEADS), _heads(i_in, HG_HEADS)
    lb_all = jnp.cumsum(jax.nn.softmax(p["hg_lb"].astype(jnp.float32), axis=0), axis=0)
    lb = lb_all[layer] - lb_all[0]
    ks, gs = [], []
    for d in range(2):
        z = (h @ p["hg_w_f"][j, d] + p["hg_b_f"][j, d]).astype(jnp.float32)
        f = lb + (1.0 - lb) * jax.nn.sigmoid(z)
        ks.append(_heads(1.0 - f, HG_HEADS))
        gs.append(_heads(jnp.log(f), HG_HEADS))
    o, s = _bidir_linear(q, ks, v, gs, s0)
    o = _norm(_merge(o), p["hg_norm_w"][j], False) * jax.nn.silu(g)
    return o @ p["hg_w_out"][j], s


def _swiglu(x, wg, wu, wd):
    return (jax.nn.silu(x @ wg) * (x @ wu)) @ wd


def _moe(x, j, p):
    b, t, d = x.shape
    xt = x.reshape(b * t, d)
    logits = (xt @ p["moe_router"][j]).astype(jnp.float32)
    top_v, top_i = lax.top_k(logits, TOP_K)
    w = jax.nn.softmax(top_v, axis=-1)
    gates = jnp.einsum("nk,nke->ne", w, jax.nn.one_hot(top_i, N_EXPERTS, dtype=jnp.float32)).astype(x.dtype)
    y = jnp.zeros_like(xt)
    for e in range(N_EXPERTS):
        y = y + gates[:, e:e + 1] * _swiglu(xt, p["moe_w_gate"][j, e], p["moe_w_up"][j, e], p["moe_w_down"][j, e])
    return y.reshape(b, t, d)


def _trunk(x, cvec, st, p, pos):
    silu_c = jax.nn.silu(cvec)
    new = {name: [] for name in ("gla", "ml_c", "ml_n", "ml_m", "ret", "hg")}
    for i in range(DEPTH):
        mod = silu_c @ p["ada_w"][i] + p["ada_b"][i]
        sh1, sc1, g1, sh2, sc2, g2 = jnp.split(mod[:, None, :], 6, axis=-1)
        h = x * (1.0 + sc1) + sh1
        kind, j = i % N_MIXERS, i // N_MIXERS
        if kind == 0:
            out, s = _gla_mixer(h, j, p, st["gla"][:, j])
            new["gla"].append(s)
        elif kind == 1:
            out, (sc_, sn_, sm_) = _mlstm_mixer(h, j, p, st["ml_c"][:, j], st["ml_n"][:, j], st["ml_m"][:, j])
            new["ml_c"].append(sc_)
            new["ml_n"].append(sn_)
            new["ml_m"].append(sm_)
        elif kind == 2:
            out, s = _ret_mixer(h, j, p, st["ret"][:, j], pos)
            new["ret"].append(s)
        else:
            out, s = _hgrn_mixer(h, j, i, p, st["hg"][:, j])
            new["hg"].append(s)
        x = _layer_norm(ALPHA * x + g1 * out, p["ln_g"][i, 0], p["ln_b"][i, 0])
        h = x * (1.0 + sc2) + sh2
        if i % 2 == 0:
            jd = i // 2
            f = _swiglu(h, p["ffn_w_gate"][jd], p["ffn_w_up"][jd], p["ffn_w_down"][jd])
        else:
            f = _moe(h, i // 2, p)
        x = _layer_norm(ALPHA * x + g2 * f, p["ln_g"][i, 1], p["ln_b"][i, 1])
    return x, {name: jnp.stack(v, axis=1) for name, v in new.items()}


def setup_inputs(seed: int = 0) -> dict:
    key = jax.random.key(seed)
    ks = iter(jax.random.split(key, 64))

    def nrm(shape, scale):
        return jax.random.normal(next(ks), shape, jnp.float32) * scale

    D = D_MODEL
    inv = D ** -0.5
    inp = {}
    inp["x_prompt"] = nrm((BATCH, SEQ, D), 1.0)
    inp["x_sample"] = nrm((DEC_BATCH, DEC_SEQ, D), 1.0)
    inp["state_gla"] = nrm((DEC_BATCH, N_GLA, 2, GLA_HEADS, GLA_DK, GLA_DV), 0.3)
    inp["state_mlstm_c"] = nrm((DEC_BATCH, N_MLSTM, 2, ML_HEADS, ML_D, ML_D), 0.3)
    inp["state_mlstm_n"] = nrm((DEC_BATCH, N_MLSTM, 2, ML_HEADS, ML_D), 0.3)
    inp["state_mlstm_m"] = nrm((DEC_BATCH, N_MLSTM, 2, ML_HEADS), 1.0)
    inp["state_ret"] = nrm((DEC_BATCH, N_RET, 2, RET_HEADS, RET_DK, RET_DV), 0.3)
    inp["state_hgrn"] = nrm((DEC_BATCH, N_HGRN, 2, HG_HEADS, HG_DF, HG_DI), 0.3)
    inp["c"] = nrm((DEC_BATCH, D), 1.0)
    inp["c_ctx"] = nrm((D,), 1.0)
    inp["ada_w"] = nrm((DEPTH, D, 6 * D), 0.5 * inv)
    inp["ada_b"] = nrm((DEPTH, 6 * D), 0.02)
    inp["ln_g"] = 1.0 + nrm((DEPTH, 2, D), 0.02)
    inp["ln_b"] = nrm((DEPTH, 2, D), 0.02)
    inp["gla_w_in"] = nrm((N_GLA, D, 2 * GLA_HEADS * GLA_DK + 2 * GLA_HEADS * GLA_DV), inv)
    inp["gla_w_g1"] = nrm((N_GLA, 2, D, GLA_RANK), inv)
    inp["gla_w_g2"] = nrm((N_GLA, 2, GLA_RANK, GLA_HEADS * GLA_DK), GLA_RANK ** -0.5)
    inp["gla_b_g"] = nrm((N_GLA, 2, GLA_HEADS * GLA_DK), 0.1)
    inp["gla_norm_w"] = 1.0 + nrm((N_GLA, GLA_DV), 0.02)
    inp["gla_w_out"] = nrm((N_GLA, GLA_HEADS * GLA_DV, D), (GLA_HEADS * GLA_DV) ** -0.5 * BETA)
    inp["ml_w_in"] = nrm((N_MLSTM, D, 4 * D), inv)
    inp["ml_w_gate"] = nrm((N_MLSTM, 2, D, 2 * ML_HEADS), inv)
    f_bias = jnp.linspace(3.0, 6.0, ML_HEADS, dtype=jnp.float32)
    inp["ml_b_gate"] = jnp.concatenate(
        [nrm((N_MLSTM, 2, ML_HEADS), 0.1), f_bias + nrm((N_MLSTM, 2, ML_HEADS), 0.1)], axis=-1)
    inp["ml_norm_w"] = 1.0 + nrm((N_MLSTM, D), 0.02)
    inp["ml_w_out"] = nrm((N_MLSTM, D, D), inv * BETA)
    inp["ret_w_in"] = nrm((N_RET, D, 2 * RET_HEADS * RET_DK + 2 * RET_HEADS * RET_DV), inv)
    gamma_logit = jnp.log(jnp.power(2.0, 5.0 + jnp.arange(RET_HEADS, dtype=jnp.float32)) - 1.0)
    inp["ret_decay"] = gamma_logit + nrm((N_RET, 2, RET_HEADS), 0.1)
    inp["ret_norm_w"] = 1.0 + nrm((N_RET, RET_HEADS * RET_DV), 0.02)
    inp["ret_w_out"] = nrm((N_RET, RET_HEADS * RET_DV, D), (RET_HEADS * RET_DV) ** -0.5 * BETA)
    inp["hg_w_in"] = nrm((N_HGRN, D, HG_HEADS * HG_DF + 2 * HG_HEADS * HG_DI), inv)
    inp["hg_w_f"] = nrm((N_HGRN, 2, D, HG_HEADS * HG_DF), inv)
    inp["hg_b_f"] = nrm((N_HGRN, 2, HG_HEADS * HG_DF), 0.1)
    inp["hg_lb"] = nrm((DEPTH, HG_HEADS * HG_DF), 0.5)
    inp["hg_norm_w"] = 1.0 + nrm((N_HGRN, D), 0.02)
    inp["hg_w_out"] = nrm((N_HGRN, D, D), inv * BETA)
    inp["ffn_w_gate"] = nrm((N_DENSE, D, D_FF), inv)
    inp["ffn_w_up"] = nrm((N_DENSE, D, D_FF), inv)
    inp["ffn_w_down"] = nrm((N_DENSE, D_FF, D), D_FF ** -0.5 * BETA)
    inp["moe_router"] = nrm((N_MOE, D, N_EXPERTS), inv)
    inp["moe_w_gate"] = nrm((N_MOE, N_EXPERTS, D, D_FF), inv)
    inp["moe_w_up"] = nrm((N_MOE, N_EXPERTS, D, D_FF), inv)
    inp["moe_w_down"] = nrm((N_MOE, N_EXPERTS, D_FF, D), D_FF ** -0.5 * BETA)
    return inp


def reference(x_prompt, x_sample, state_gla, state_mlstm_c, state_mlstm_n, state_mlstm_m, state_ret,
              state_hgrn, c, c_ctx, ada_w, ada_b, ln_g, ln_b, gla_w_in, gla_w_g1, gla_w_g2, gla_b_g,
              gla_norm_w, gla_w_out, ml_w_in, ml_w_gate, ml_b_gate, ml_norm_w, ml_w_out, ret_w_in,
              ret_decay, ret_norm_w, ret_w_out, hg_w_in, hg_w_f, hg_b_f, hg_lb, hg_norm_w, hg_w_out,
              ffn_w_gate, ffn_w_up, ffn_w_down, moe_router, moe_w_gate, moe_w_up, moe_w_down):
    p = dict(ada_w=ada_w, ada_b=ada_b, ln_g=ln_g, ln_b=ln_b,
             gla_w_in=gla_w_in, gla_w_g1=gla_w_g1, gla_w_g2=gla_w_g2, gla_b_g=gla_b_g,
             gla_norm_w=gla_norm_w, gla_w_out=gla_w_out,
             ml_w_in=ml_w_in, ml_w_gate=ml_w_gate, ml_b_gate=ml_b_gate, ml_norm_w=ml_norm_w,
             ml_w_out=ml_w_out,
             ret_w_in=ret_w_in, ret_decay=ret_decay, ret_norm_w=ret_norm_w, ret_w_out=ret_w_out,
             hg_w_in=hg_w_in, hg_w_f=hg_w_f, hg_b_f=hg_b_f, hg_lb=hg_lb, hg_norm_w=hg_norm_w,
             hg_w_out=hg_w_out,
             ffn_w_gate=ffn_w_gate, ffn_w_up=ffn_w_up, ffn_w_down=ffn_w_down,
             moe_router=moe_router, moe_w_gate=moe_w_gate, moe_w_up=moe_w_up, moe_w_down=moe_w_down)

    bp = x_prompt.shape[0]
    zeros = lambda *shape: jnp.zeros(shape, x_prompt.dtype)
    ctx_init = dict(gla=zeros(bp, N_GLA, 2, GLA_HEADS, GLA_DK, GLA_DV),
                    ml_c=zeros(bp, N_MLSTM, 2, ML_HEADS, ML_D, ML_D),
                    ml_n=zeros(bp, N_MLSTM, 2, ML_HEADS, ML_D),
                    ml_m=zeros(bp, N_MLSTM, 2, ML_HEADS),
                    ret=zeros(bp, N_RET, 2, RET_HEADS, RET_DK, RET_DV),
                    hg=zeros(bp, N_HGRN, 2, HG_HEADS, HG_DF, HG_DI))
    y_prompt, ns = _trunk(x_prompt, c_ctx[None, :], ctx_init, p, None)

    t = x_sample.shape[1]
    ROWS = t // GRID_W
    rows = jnp.repeat(jnp.arange(ROWS, dtype=jnp.float32), GRID_W)
    cols = jnp.tile(jnp.arange(GRID_W, dtype=jnp.float32), ROWS)
    lat_init = dict(gla=state_gla, ml_c=state_mlstm_c, ml_n=state_mlstm_n, ml_m=state_mlstm_m,
                    ret=state_ret, hg=state_hgrn)
    y_sample, _ = _trunk(x_sample, c, lat_init, p, (rows, cols))

    return (y_prompt, y_sample, ns["gla"], ns["ml_c"], ns["ml_n"], ns["ml_m"], ns["ret"], ns["hg"])
```

```python
import functools

import jax
import jax.numpy as jnp
from jax import lax
from jax.experimental import pallas as pl
from jax.experimental.pallas import tpu as pltpu

F32 = jnp.float32
BF16 = jnp.bfloat16
HIGHEST = lax.Precision.HIGHEST

D_MODEL = 1024
DEPTH = 4
GRID_W = 64
ALPHA = (2.0 * DEPTH) ** 0.25
EPS = 1e-5
ROPE_BASE = 10000.0
GLA_HEADS, GLA_DK, GLA_DV, GLA_RANK, GLA_TAU = 4, 128, 256, 16, 16.0
ML_HEADS, ML_D = 4, 256
RET_HEADS, RET_DK, RET_DV = 4, 256, 512
HG_HEADS, HG_DF, HG_DI = 8, 128, 128
D_FF = 2816
N_EXPERTS = 8

V7X_VMEM_LIMIT_BYTES = 56 * 1024 * 1024
LANES = 128
SUBLANES = 8

SCALAR_CHUNK = 256
PDIM_CHUNK = 64


def _cparams(n_axes):
    return pltpu.CompilerParams(dimension_semantics=("arbitrary",) * n_axes,
                                vmem_limit_bytes=V7X_VMEM_LIMIT_BYTES)


def _sigmoid(x):
    return jax.nn.sigmoid(x)


def _silu(x):
    return x * jax.nn.sigmoid(x)


def _log_sigmoid(x):
    return jnp.minimum(x, 0.0) - jnp.log1p(jnp.exp(-jnp.abs(x)))


def _dot(a, b):
    return jnp.dot(a, b, preferred_element_type=F32)


def _dot_nt(a, b):
    return lax.dot_general(a, b, (((1,), (1,)), ((), ())), preferred_element_type=F32)


def _dot_tn(a, b):
    return lax.dot_general(a, b, (((0,), (0,)), ((), ())), preferred_element_type=F32)


def _largest_tile(n, cap):
    best = None
    for t in range(LANES, min(n, cap) + 1, LANES):
        if n % t == 0:
            best = t
    assert best is not None, (n, cap)
    return best


class _Tokens:
    def __init__(self, n_ctx, lat_seq):
        self.n_ctx, self.lat_seq = n_ctx, lat_seq

    def group(self, i, tm):
        r = i * tm
        return jnp.where(r < self.n_ctx, 0, 1 + (r - self.n_ctx) // self.lat_seq)


def _linear_kernel(*refs, pro, epi, has_bias):
    it = iter(refs)
    x_ref = next(it)
    sh_ref = sc_ref = b_ref = None
    if pro == "mod":
        sh_ref, sc_ref = next(it), next(it)
    w_ref = next(it)
    if has_bias:
        b_ref = next(it)
    o_ref = next(it)
    h_scr = next(it)

    @pl.when(pl.program_id(1) == 0)
    def _():
        x = x_ref[...].astype(F32)
        if pro == "mod":
            x = x * (1.0 + sc_ref[...]) + sh_ref[...]
        elif pro == "silu":
            x = _silu(x)
        h_scr[...] = x.astype(BF16)

    acc = _dot(h_scr[...], w_ref[...].astype(BF16))
    if has_bias:
        acc = acc + b_ref[...]
    if epi == "logsig_tau":
        acc = _log_sigmoid(acc) / GLA_TAU
    o_ref[...] = acc.astype(o_ref.dtype)


def _linear(x, w, *, widx=0, bias=None, mod=None, mod_cols=None, tok=None, pro=None, epi=None,
            tm, tn_cap=1536, out_dtype=F32):
    m, k = x.shape
    w3 = w if w.ndim == 3 else w.reshape(1, *w.shape)
    n = w3.shape[-1]
    tn = _largest_tile(n, tn_cap)
    assert m % tm == 0
    in_specs = [pl.BlockSpec((tm, k), lambda i, j: (i, 0))]
    args = [x]
    if pro == "mod":
        for c in mod_cols:
            in_specs.append(pl.BlockSpec((None, None, 1, k),
                                         lambda i, j, c=c: (tok.group(i, tm), c, 0, 0)))
            args.append(mod)
    in_specs.append(pl.BlockSpec((None, k, tn), lambda i, j: (widx, 0, j)))
    args.append(w3)
    if bias is not None:
        in_specs.append(pl.BlockSpec((1, tn), lambda i, j: (0, j)))
        args.append(bias.reshape(1, n).astype(F32))
    return pl.pallas_call(
        functools.partial(_linear_kernel, pro=pro, epi=epi, has_bias=bias is not None),
        grid=(m // tm, n // tn),
        in_specs=in_specs,
        out_specs=pl.BlockSpec((tm, tn), lambda i, j: (i, j)),
        out_shape=jax.ShapeDtypeStruct((m, n), out_dtype),
        scratch_shapes=[pltpu.VMEM((tm, k), BF16)],
        compiler_params=_cparams(2),
        name="linear",
    )(*args)


FFN_COL_CHUNK = 256


def _ffn_up_kernel(e_ref, *refs, pro):
    it = iter(refs)
    x_ref = next(it)
    sh_ref = sc_ref = None
    if pro == "mod":
        sh_ref, sc_ref = next(it), next(it)
    wg_ref, wu_ref, o_ref, h_scr = next(it), next(it), next(it), next(it)

    @pl.when(pl.program_id(1) == 0)
    def _():
        x = x_ref[...].astype(F32)
        if pro == "mod":
            x = x * (1.0 + sc_ref[...]) + sh_ref[...]
        h_scr[...] = x.astype(BF16)

    h = h_scr[...]
    tf = o_ref.shape[-1]
    for c0 in range(0, tf, FFN_COL_CHUNK):
        c1 = min(c0 + FFN_COL_CHUNK, tf)
        g = _dot(h, wg_ref[:, c0:c1].astype(BF16))
        u = _dot(h, wu_ref[:, c0:c1].astype(BF16))
        o_ref[:, c0:c1] = (_silu(g) * u).astype(o_ref.dtype)


def _ffn_up(x, wg, wu, tile_expert, *, mod=None, tok=None, tm, tf):
    m, k = x.shape
    f = wg.shape[-1]
    pro = "mod" if mod is not None else None
    in_specs = [pl.BlockSpec((tm, k), lambda i, j, e: (i, 0))]
    args = [x]
    if pro == "mod":
        for c in (3, 4):
            in_specs.append(pl.BlockSpec((None, None, 1, k),
                                         lambda i, j, e, c=c: (tok.group(i, tm), c, 0, 0)))
            args.append(mod)
    wspec = pl.BlockSpec((None, k, tf), lambda i, j, e: (e[i], 0, j))
    in_specs += [wspec, wspec]
    args += [wg, wu]
    return pl.pallas_call(
        functools.partial(_ffn_up_kernel, pro=pro),
        grid_spec=pltpu.PrefetchScalarGridSpec(
            num_scalar_prefetch=1,
            grid=(m // tm, f // tf),
            in_specs=in_specs,
            out_specs=pl.BlockSpec((tm, tf), lambda i, j, e: (i, j)),
            scratch_shapes=[pltpu.VMEM((tm, k), BF16)],
        ),
        out_shape=jax.ShapeDtypeStruct((m, f), BF16),
        compiler_params=_cparams(2),
        name="ffn_up",
    )(tile_expert, *args)


def _layer_norm_rows(x, g, b):
    x = x - jnp.mean(x, axis=-1, keepdims=True)
    x = x * lax.rsqrt(jnp.mean(x * x, axis=-1, keepdims=True) + EPS)
    return x * g + b


def _out_kernel(e_ref, *refs, pro, epi, nk):
    it = iter(refs)
    y_ref = next(it)
    gin_ref = nw_ref = x_ref = gate_ref = lg_ref = lb_ref = None
    if pro == "hgrn":
        gin_ref, nw_ref = next(it), next(it)
    w_ref = next(it)
    if epi == "resln":
        x_ref, gate_ref, lg_ref, lb_ref = next(it), next(it), next(it), next(it)
    o_ref = next(it)
    kk = pl.program_id(1)

    y = y_ref[...]
    if pro == "hgrn":
        y = y.astype(F32)
        y = y * lax.rsqrt(jnp.mean(y * y, axis=-1, keepdims=True) + EPS)
        y = y * nw_ref[...] * _silu(gin_ref[...])
    part = _dot(y.astype(BF16), w_ref[...].astype(BF16))

    if nk > 1:
        @pl.when(kk == 0)
        def _():
            o_ref[...] = part

        @pl.when(kk > 0)
        def _():
            o_ref[...] += part

    def finish():
        acc = part if nk == 1 else o_ref[...]
        if epi == "resln":
            acc = _layer_norm_rows(ALPHA * x_ref[...] + gate_ref[...] * acc, lg_ref[...], lb_ref[...])
        o_ref[...] = acc

    if nk == 1:
        finish()
    elif epi == "resln":
        pl.when(kk == nk - 1)(finish)


def _out_proj(y, w, tile_expert, *, tk, tm, resln=None, hgrn=None, tok=None):
    m, k = y.shape
    d = w.shape[-1]
    nk = k // tk
    assert k % tk == 0 and (hgrn is None or nk == 1)
    in_specs = [pl.BlockSpec((tm, tk), lambda i, kk, e: (i, kk))]
    args = [y]
    if hgrn is not None:
        proj, gblk, nw = hgrn
        in_specs += [pl.BlockSpec((tm, k), lambda i, kk, e: (i, gblk)),
                     pl.BlockSpec((1, k), lambda i, kk, e: (0, 0))]
        args += [proj, nw.reshape(1, k)]
    in_specs.append(pl.BlockSpec((None, tk, d), lambda i, kk, e: (e[i], kk, 0)))
    args.append(w)
    if resln is not None:
        x, mod, gcol, lg, lb = resln
        in_specs += [pl.BlockSpec((tm, d), lambda i, kk, e: (i, 0)),
                     pl.BlockSpec((None, None, 1, d), lambda i, kk, e: (tok.group(i, tm), gcol, 0, 0)),
                     pl.BlockSpec((1, d), lambda i, kk, e: (0, 0)),
                     pl.BlockSpec((1, d), lambda i, kk, e: (0, 0))]
        args += [x, mod, lg.reshape(1, d), lb.reshape(1, d)]
    return pl.pallas_call(
        functools.partial(_out_kernel, pro="hgrn" if hgrn is not None else None,
                          epi="resln" if resln is not None else None, nk=nk),
        grid_spec=pltpu.PrefetchScalarGridSpec(
            num_scalar_prefetch=1,
            grid=(m // tm, nk),
            in_specs=in_specs,
            out_specs=pl.BlockSpec((tm, d), lambda i, kk, e: (i, 0)),
        ),
        out_shape=jax.ShapeDtypeStruct((m, d), F32),
        compiler_params=_cparams(2),
        name="out_proj",
    )(tile_expert, *args)


def _router_kernel(x_ref, sh_ref, sc_ref, wr_ref, h_ref, r_ref):
    h = x_ref[...] * (1.0 + sc_ref[...]) + sh_ref[...]
    h_ref[...] = h
    logits = _dot(h.astype(BF16), wr_ref[...].astype(BF16))
    lane = lax.broadcasted_iota(jnp.int32, logits.shape, 1)
    neg = jnp.float32(-jnp.inf)
    logits = jnp.where(lane < N_EXPERTS, logits, neg)
    m1 = jnp.max(logits, axis=-1, keepdims=True)
    i1 = jnp.min(jnp.where(logits == m1, lane, LANES), axis=-1, keepdims=True)
    rest = jnp.where(lane == i1, neg, logits)
    m2 = jnp.max(rest, axis=-1, keepdims=True)
    i2 = jnp.min(jnp.where(rest == m2, lane, LANES), axis=-1, keepdims=True)
    e2 = jnp.exp(m2 - m1)
    w1 = 1.0 / (1.0 + e2)
    w2 = e2 / (1.0 + e2)
    out = jnp.where(lane == 0, i1.astype(F32), 0.0)
    out = jnp.where(lane == 1, i2.astype(F32), out)
    out = jnp.where(lane == 2, w1, out)
    out = jnp.where(lane == 3, w2, out)
    r_ref[...] = out


def _router(x, mod, wr_pad, *, widx, tok, tm):
    m, k = x.shape
    wr3 = wr_pad
    return pl.pallas_call(
        _router_kernel,
        grid=(m // tm,),
        in_specs=[pl.BlockSpec((tm, k), lambda i: (i, 0)),
                  pl.BlockSpec((None, None, 1, k), lambda i: (tok.group(i, tm), 3, 0, 0)),
                  pl.BlockSpec((None, None, 1, k), lambda i: (tok.group(i, tm), 4, 0, 0)),
                  pl.BlockSpec((None, k, LANES), lambda i: (widx, 0, 0))],
        out_specs=[pl.BlockSpec((tm, k), lambda i: (i, 0)),
                   pl.BlockSpec((tm, LANES), lambda i: (i, 0))],
        out_shape=[jax.ShapeDtypeStruct((m, k), F32), jax.ShapeDtypeStruct((m, LANES), F32)],
        compiler_params=_cparams(1),
        name="router",
    )(x, mod, mod, wr3)


def _row_copy(src_ref, dst_ref, sem, src_row, dst_row):
    return pltpu.make_async_copy(src_ref.at[pl.ds(src_row, 1), :], dst_ref.at[pl.ds(dst_row, 1), :], sem)


def _gather_kernel(idx_ref, src_ref, o_ref, sem, *, rows):
    base = pl.program_id(0) * rows

    def start(r, c):
        _row_copy(src_ref, o_ref, sem, idx_ref[base + r], r).start()
        return c

    lax.fori_loop(0, rows, start, 0)

    def wait(r, c):
        _row_copy(src_ref, o_ref, sem, 0, r).wait()
        return c

    lax.fori_loop(0, rows, wait, 0)


def _gather_rows(src, idx, *, rows):
    n_out = idx.shape[0]
    d = src.shape[1]
    return pl.pallas_call(
        functools.partial(_gather_kernel, rows=rows),
        grid_spec=pltpu.PrefetchScalarGridSpec(
            num_scalar_prefetch=1,
            grid=(n_out // rows,),
            in_specs=[pl.BlockSpec(memory_space=pl.ANY)],
            out_specs=pl.BlockSpec((rows, d), lambda i, idx: (i, 0)),
            scratch_shapes=[pltpu.SemaphoreType.DMA(())],
        ),
        out_shape=jax.ShapeDtypeStruct((n_out, d), src.dtype),
        compiler_params=_cparams(1),
        name="gather_rows",
    )(idx, src)


def _combine_kernel(p1_ref, p2_ref, ys_ref, r_ref, x_ref, gate_ref, lg_ref, lb_ref, o_ref,
                    a_scr, b_scr, sem, *, rows):
    base = pl.program_id(0) * rows

    def start(r, c):
        _row_copy(ys_ref, a_scr, sem.at[0], p1_ref[base + r], r).start()
        _row_copy(ys_ref, b_scr, sem.at[1], p2_ref[base + r], r).start()
        return c

    lax.fori_loop(0, rows, start, 0)

    def wait(r, c):
        _row_copy(ys_ref, a_scr, sem.at[0], 0, r).wait()
        _row_copy(ys_ref, b_scr, sem.at[1], 0, r).wait()
        return c

    lax.fori_loop(0, rows, wait, 0)
    rt = r_ref[...]
    f = rt[:, 2:3] * a_scr[...] + rt[:, 3:4] * b_scr[...]
    o_ref[...] = _layer_norm_rows(ALPHA * x_ref[...] + gate_ref[...] * f, lg_ref[...], lb_ref[...])


def _moe_combine(ys, route, p1, p2, x, mod, lg, lb, *, tok, rows):
    m, d = x.shape
    return pl.pallas_call(
        functools.partial(_combine_kernel, rows=rows),
        grid_spec=pltpu.PrefetchScalarGridSpec(
            num_scalar_prefetch=2,
            grid=(m // rows,),
            in_specs=[pl.BlockSpec(memory_space=pl.ANY),
                      pl.BlockSpec((rows, LANES), lambda i, p1, p2: (i, 0)),
                      pl.BlockSpec((rows, d), lambda i, p1, p2: (i, 0)),
                      pl.BlockSpec((None, None, 1, d), lambda i, p1, p2: (tok.group(i, rows), 5, 0, 0)),
                      pl.BlockSpec((1, d), lambda i, p1, p2: (0, 0)),
                      pl.BlockSpec((1, d), lambda i, p1, p2: (0, 0))],
            out_specs=pl.BlockSpec((rows, d), lambda i, p1, p2: (i, 0)),
            scratch_shapes=[pltpu.VMEM((rows, d), F32), pltpu.VMEM((rows, d), F32),
                            pltpu.SemaphoreType.DMA((2,))],
        ),
        out_shape=jax.ShapeDtypeStruct((m, d), F32),
        compiler_params=_cparams(1),
        name="moe_combine",
    )(p1, p2, ys, route, x, mod, lg.reshape(1, d), lb.reshape(1, d))


def _ret_kernel(dec_ref, *refs, seq, has_init, want_state, rope):
    it = iter(refs)
    q_ref, k_ref, v_ref, g_ref, nw_ref = (next(it) for _ in range(5))
    cos_ref = sin_ref = s0_ref = sfin_ref = None
    if rope:
        cos_ref, sin_ref = next(it), next(it)
    if has_init:
        s0_ref = next(it)
    y_ref = next(it)
    if want_state:
        sfin_ref = next(it)
    q_scr, k_scr, o_scr, sf_scr, sb_scr = (next(it) for _ in range(5))

    c = SCALAR_CHUNK
    nchunks = seq // c
    h = pl.program_id(1)
    lgf = _log_sigmoid(jnp.full((1, 1), dec_ref[h], F32))
    lgb = _log_sigmoid(jnp.full((1, 1), dec_ref[RET_HEADS + h], F32))

    ii = lax.broadcasted_iota(jnp.int32, (c, c), 0)
    jj = lax.broadcasted_iota(jnp.int32, (c, c), 1)
    dlt = (ii - jj).astype(F32)
    decay = (jnp.where(dlt >= 0, jnp.exp(lgf * jnp.maximum(dlt, 0.0)), 0.0)
             + jnp.where(dlt <= 0, jnp.exp(lgb * jnp.maximum(-dlt, 0.0)), 0.0))
    pos = lax.broadcasted_iota(jnp.int32, (c, 1), 0).astype(F32)
    q_in_f = jnp.exp(lgf * (pos + 1.0))
    q_in_b = jnp.exp(lgb * (c - pos))
    k_out_f = jnp.exp(lgf * (c - 1.0 - pos))
    k_out_b = jnp.exp(lgb * pos)
    tot_f = jnp.exp(lgf * c)
    tot_b = jnp.exp(lgb * c)

    half = RET_DK // 2
    for n in range(nchunks):
        rows = slice(n * c, (n + 1) * c)
        q = q_ref[0, rows, :]
        k = k_ref[0, rows, :] * (RET_DK ** -0.5)
        if rope:
            cs, sn = cos_ref[rows, :], sin_ref[rows, :]
            q = jnp.concatenate([q[:, :half] * cs - q[:, half:] * sn, q[:, :half] * sn + q[:, half:] * cs], axis=1)
            k = jnp.concatenate([k[:, :half] * cs - k[:, half:] * sn, k[:, :half] * sn + k[:, half:] * cs], axis=1)
        q_scr[rows, :] = q
        k_scr[rows, :] = k

    if has_init:
        sf_scr[...] = s0_ref[0, 0, 0]
        sb_scr[...] = s0_ref[0, 1, 0]

    for n in range(nchunks):
        rows = slice(n * c, (n + 1) * c)
        q, k = q_scr[rows, :], k_scr[rows, :]
        vb = v_ref[0, rows, :].astype(BF16)
        att = _dot_nt(q.astype(BF16), k.astype(BF16)) * decay
        o = _dot(att.astype(BF16), vb)
        live = has_init or n > 0
        if live:
            o = o + _dot((q * q_in_f).astype(BF16), sf_scr[...].astype(BF16))
        o_scr[rows, :] = o
        if n < nchunks - 1 or want_state:
            upd = _dot_tn((k * k_out_f).astype(BF16), vb)
            sf_scr[...] = tot_f * sf_scr[...] + upd if live else upd

    for n in reversed(range(nchunks)):
        rows = slice(n * c, (n + 1) * c)
        q, k = q_scr[rows, :], k_scr[rows, :]
        live = has_init or n < nchunks - 1
        if live:
            o_scr[rows, :] += _dot((q * q_in_b).astype(BF16), sb_scr[...].astype(BF16))
        if n > 0 or want_state:
            upd = _dot_tn((k * k_out_b).astype(BF16), v_ref[0, rows, :].astype(BF16))
            sb_scr[...] = tot_b * sb_scr[...] + upd if live else upd

    if want_state:
        sfin_ref[0, 0, 0] = sf_scr[...]
        sfin_ref[0, 1, 0] = sb_scr[...]

    for n in range(nchunks):
        rows = slice(n * c, (n + 1) * c)
        o = o_scr[rows, :]
        o = o - jnp.mean(o, axis=-1, keepdims=True)
        o = o * lax.rsqrt(jnp.mean(o * o, axis=-1, keepdims=True) + EPS)
        y_ref[0, rows, :] = (o * nw_ref[...] * _silu(g_ref[0, rows, :])).astype(y_ref.dtype)


def _ret_scan(proj, decay, norm_w, *, b, boff, s0=None, want_state, rope_tabs=None):
    _, t, _ = proj.shape
    hh, dk, dv = RET_HEADS, RET_DK, RET_DV
    koff, voff, goff = hh * dk // dk, 2 * hh * dk // dv, (2 * hh * dk + hh * dv) // dv
    in_specs = [pl.BlockSpec((1, t, dk), lambda i, h, d: (boff + i, 0, h)),
                pl.BlockSpec((1, t, dk), lambda i, h, d: (boff + i, 0, koff + h)),
                pl.BlockSpec((1, t, dv), lambda i, h, d: (boff + i, 0, voff + h)),
                pl.BlockSpec((1, t, dv), lambda i, h, d: (boff + i, 0, goff + h)),
                pl.BlockSpec((1, dv), lambda i, h, d: (0, h))]
    args = [proj, proj, proj, proj, norm_w.reshape(1, hh * dv)]
    if rope_tabs is not None:
        in_specs += [pl.BlockSpec((t, dk // 2), lambda i, h, d: (0, 0))] * 2
        args += list(rope_tabs)
    if s0 is not None:
        in_specs.append(pl.BlockSpec((1, 2, 1, dk, dv), lambda i, h, d: (i, 0, h, 0, 0)))
        args.append(s0)
    out_specs = [pl.BlockSpec((1, t, dv), lambda i, h, d: (i, 0, h))]
    out_shape = [jax.ShapeDtypeStruct((b, t, hh * dv), BF16)]
    if want_state:
        out_specs.append(pl.BlockSpec((1, 2, 1, dk, dv), lambda i, h, d: (i, 0, h, 0, 0)))
        out_shape.append(jax.ShapeDtypeStruct((b, 2, hh, dk, dv), F32))
    res = pl.pallas_call(
        functools.partial(_ret_kernel, seq=t, has_init=s0 is not None, want_state=want_state,
                          rope=rope_tabs is not None),
        grid_spec=pltpu.PrefetchScalarGridSpec(
            num_scalar_prefetch=1,
            grid=(b, hh),
            in_specs=in_specs,
            out_specs=out_specs,
            scratch_shapes=[pltpu.VMEM((t, dk), F32), pltpu.VMEM((t, dk), F32), pltpu.VMEM((t, dv), F32),
                            pltpu.VMEM((dk, dv), F32), pltpu.VMEM((dk, dv), F32)],
        ),
        out_shape=out_shape,
        compiler_params=_cparams(2),
        name="ret_scan",
    )(decay.reshape(2 * hh).astype(F32), *args)
    return res if want_state else (res[0], None)


def _mlstm_dir(q, qb, k, vb, s, cum_col, cum_row, ib_col, ib_row, mask, edge, c_ref, n_ref, m_ref,
               update_state):
    m_prev = m_ref[...]
    logw = jnp.where(mask, cum_col + (ib_row - cum_row), -jnp.inf)
    log_prev = cum_col + m_prev
    m_t = jnp.maximum(log_prev, jnp.max(logw, axis=1, keepdims=True))
    w_prev = jnp.exp(log_prev - m_t)
    qk = s * jnp.exp(logw - m_t)
    num = w_prev * _dot(qb, c_ref[...].astype(BF16)) + _dot(qk.astype(BF16), vb)
    den = w_prev * jnp.sum(q * n_ref[...], axis=1, keepdims=True) + jnp.sum(qk, axis=1, keepdims=True)
    out = num / jnp.maximum(jnp.abs(den), jnp.exp(-m_t))
    if update_state:
        m_new = m_t[edge:edge + 1, :]
        cum_edge = cum_col[edge:edge + 1, :]
        w_end = jnp.exp(cum_edge - cum_col + ib_col - m_new)
        dec = jnp.exp(cum_edge + m_prev - m_new)
        kw = k * w_end
        c_ref[...] = dec * c_ref[...] + _dot_tn(kw.astype(BF16), vb)
        n_ref[...] = dec * n_ref[...] + jnp.sum(kw, axis=0, keepdims=True)
        m_ref[...] = m_new
    return out


def _mlstm_kernel(*refs, seq, has_init, want_state):
    it = iter(refs)
    q_ref, k_ref, v_ref, og_ref, z_ref, nw_ref = (next(it) for _ in range(6))
    c0_ref = n0_ref = m0_ref = cf_ref = nf_ref = mf_ref = None
    if has_init:
        c0_ref, n0_ref, m0_ref = next(it), next(it), next(it)
    y_ref = next(it)
    if want_state:
        cf_ref, nf_ref, mf_ref = next(it), next(it), next(it)
    o_scr, c_scr, n_scr, m_scr = (next(it) for _ in range(4))

    c = SCALAR_CHUNK
    nchunks = seq // c
    ii = lax.broadcasted_iota(jnp.int32, (c, c), 0)
    jj = lax.broadcasted_iota(jnp.int32, (c, c), 1)
    lower = ii >= jj
    upper = ii <= jj
    tri_lo = lower.astype(F32)
    tri_up = upper.astype(F32)

    for d in range(2):
        if has_init:
            c_scr[d] = c0_ref[0, d, 0]
            n_scr[d] = n0_ref[0, d, 0]
            m_scr[d] = m0_ref[0, d, 0][:, :1]
        else:
            c_scr[d] = jnp.zeros(c_scr.shape[1:], F32)
            n_scr[d] = jnp.zeros(n_scr.shape[1:], F32)
            m_scr[d] = jnp.zeros(m_scr.shape[1:], F32)

    def chunk(n, d):
        rows = slice(n * c, (n + 1) * c)
        q = q_ref[0, rows, :]
        k = k_ref[0, rows, :] * (ML_D ** -0.5)
        qb, kb, vb = q.astype(BF16), k.astype(BF16), v_ref[0, rows, :].astype(BF16)
        s = _dot_nt(qb, kb)
        z = z_ref[0, rows, :]
        zt = z.T
        lf, lft = _log_sigmoid(z), _log_sigmoid(zt)
        last = (n == nchunks - 1) if d == 0 else (n == 0)
        upd = (not last) or want_state
        if d == 0:
            cum_col = jnp.dot(tri_lo, lf, precision=HIGHEST, preferred_element_type=F32)[:, 1:2]
            cum_row = jnp.dot(lft, tri_up, precision=HIGHEST, preferred_element_type=F32)[1:2, :]
            return _mlstm_dir(q, qb, k, vb, s, cum_col, cum_row, z[:, 0:1], zt[0:1, :], lower, c - 1,
                              c_scr.at[0], n_scr.at[0], m_scr.at[0], upd)
        cum_col = jnp.dot(tri_up, lf, precision=HIGHEST, preferred_element_type=F32)[:, 3:4]
        cum_row = jnp.dot(lft, tri_lo, precision=HIGHEST, preferred_element_type=F32)[3:4, :]
        return _mlstm_dir(q, qb, k, vb, s, cum_col, cum_row, z[:, 2:3], zt[2:3, :], upper, 0,
                          c_scr.at[1], n_scr.at[1], m_scr.at[1], upd)

    for n in range(nchunks):
        o_scr[n * c:(n + 1) * c, :] = chunk(n, 0)
    for n in reversed(range(nchunks)):
        o_scr[n * c:(n + 1) * c, :] += chunk(n, 1)

    if want_state:
        for d in range(2):
            cf_ref[0, d, 0] = c_scr[d]
            nf_ref[0, d, 0] = n_scr[d]
            mf_ref[0, d, 0] = jnp.broadcast_to(m_scr[d], (1, LANES))

    for n in range(nchunks):
        rows = slice(n * c, (n + 1) * c)
        o = o_scr[rows, :]
        o = o - jnp.mean(o, axis=-1, keepdims=True)
        o = o * lax.rsqrt(jnp.mean(o * o, axis=-1, keepdims=True) + EPS)
        y_ref[0, rows, :] = (o * nw_ref[...] * _sigmoid(og_ref[0, rows, :])).astype(y_ref.dtype)


def _mlstm_scan(proj, norm_w, *, b, boff, init=None, want_state):
    _, t, _ = proj.shape
    hh, d = ML_HEADS, ML_D
    zoff = 4 * hh * d // LANES
    in_specs = [pl.BlockSpec((1, t, d), lambda i, h: (boff + i, 0, h)),
                pl.BlockSpec((1, t, d), lambda i, h: (boff + i, 0, hh + h)),
                pl.BlockSpec((1, t, d), lambda i, h: (boff + i, 0, 2 * hh + h)),
                pl.BlockSpec((1, t, d), lambda i, h: (boff + i, 0, 3 * hh + h)),
                pl.BlockSpec((1, t, LANES), lambda i, h: (boff + i, 0, zoff + h)),
                pl.BlockSpec((1, d), lambda i, h: (0, h))]
    args = [proj] * 5 + [norm_w.reshape(1, hh * d)]
    st_specs = [pl.BlockSpec((1, 2, 1, d, d), lambda i, h: (i, 0, h, 0, 0)),
                pl.BlockSpec((1, 2, 1, 1, d), lambda i, h: (i, 0, h, 0, 0)),
                pl.BlockSpec((1, 2, 1, 1, LANES), lambda i, h: (i, 0, h, 0, 0))]
    if init is not None:
        in_specs += st_specs
        args += list(init)
    out_specs = [pl.BlockSpec((1, t, d), lambda i, h: (i, 0, h))]
    out_shape = [jax.ShapeDtypeStruct((b, t, hh * d), BF16)]
    if want_state:
        out_specs += st_specs
        out_shape += [jax.ShapeDtypeStruct((b, 2, hh, d, d), F32),
                      jax.ShapeDtypeStruct((b, 2, hh, 1, d), F32),
                      jax.ShapeDtypeStruct((b, 2, hh, 1, LANES), F32)]
    res = pl.pallas_call(
        functools.partial(_mlstm_kernel, seq=t, has_init=init is not None, want_state=want_state),
        grid=(b, hh),
        in_specs=in_specs,
        out_specs=out_specs,
        out_shape=out_shape,
        scratch_shapes=[pltpu.VMEM((t, d), F32), pltpu.VMEM((2, d, d), F32), pltpu.VMEM((2, 1, d), F32),
                        pltpu.VMEM((2, 1, 1), F32)],
        compiler_params=_cparams(2),
        name="mlstm_scan",
    )(*args)
    return res


def _group_bcast(x, row, group, r):
    n, d = x.shape
    if group >= SUBLANES:
        picked = x.reshape(n // group, group, d)[:, r:r + 1, :]
        return jnp.broadcast_to(picked, (n // group, group, d)).reshape(n, d)
    pos = row & (group - 1)
    out = x
    for off in range(-r, group - r):
        if off != 0:
            out = jnp.where(pos - r == off, pltpu.roll(x, off % n, 0), out)
    return out


def _prefix_rows(g, row):
    x = g
    d = 1
    while d < g.shape[0]:
        x = x + jnp.where(row >= d, pltpu.roll(x, d, 0), 0.0)
        d *= 2
    return x


def _suffix_rows(g, row):
    n = g.shape[0]
    x = g
    d = 1
    while d < n:
        x = x + jnp.where(row < n - d, pltpu.roll(x, n - d, 0), 0.0)
        d *= 2
    return x


def _pdim_intra(q, kf, kb, bf, cb, row, ii, jj):
    c = q.shape[0]
    att = jnp.where(ii == jj, _dot_nt(q.astype(BF16), (kf + kb).astype(BF16)), 0.0)
    s = 1
    while s < c:
        grp = 2 * s
        pos = row & (grp - 1)
        lo, hi = pos < s, pos >= s
        mf = _group_bcast(bf, row, grp, s - 1)
        mb = _group_bcast(cb, row, grp, s)
        qf = jnp.where(hi, q * jnp.exp(jnp.minimum(bf - mf, 0.0)), 0.0)
        kf_ = jnp.where(lo, kf * jnp.exp(jnp.minimum(mf - bf, 0.0)), 0.0)
        qb = jnp.where(lo, q * jnp.exp(jnp.minimum(cb - mb, 0.0)), 0.0)
        kb_ = jnp.where(hi, kb * jnp.exp(jnp.minimum(mb - cb, 0.0)), 0.0)
        qq = jnp.concatenate([qf, qb], axis=1).astype(BF16)
        kk = jnp.concatenate([kf_, kb_], axis=1).astype(BF16)
        att = att + jnp.where((ii ^ jj) < grp, _dot_nt(qq, kk), 0.0)
        s = grp
    return att


def _pdim_kernel(*refs, seq, mode, layer, has_init, want_state):
    it = iter(refs)
    if mode == "gla":
        q_ref, k_ref, v_ref, gf_ref, gb_ref, r_ref, nw_ref = (next(it) for _ in range(7))
    else:
        q_ref, v_ref, gf_ref, gb_ref, lbp_ref = (next(it) for _ in range(5))
    s0_ref = sfin_ref = None
    if has_init:
        s0_ref = next(it)
    y_ref = next(it)
    if want_state:
        sfin_ref = next(it)
    o_scr, sf_scr, sb_scr = (next(it) for _ in range(3))

    c = PDIM_CHUNK
    nchunks = seq // c
    row = lax.broadcasted_iota(jnp.int32, (c, 1), 0)
    ii = lax.broadcasted_iota(jnp.int32, (c, c), 0)
    jj = lax.broadcasted_iota(jnp.int32, (c, c), 1)

    if mode == "hgrn":
        p = lbp_ref[...]
        p = jnp.exp(p - jnp.max(p, axis=0, keepdims=True))
        p = p / jnp.sum(p, axis=0, keepdims=True)
        lb = jnp.zeros((1, p.shape[1]), F32)
        for r in range(1, layer + 1):
            lb = lb + p[r:r + 1, :]

    def load(n, need_f, need_b):
        rows = pl.ds(pl.multiple_of(n * c, c), c)
        v = v_ref[0, rows, :]
        if mode == "gla":
            q = q_ref[0, rows, :] * (GLA_DK ** -0.5)
            k = k_ref[0, rows, :]
            return q, k, k, (gf_ref[0, rows, :] if need_f else None), (gb_ref[0, rows, :] if need_b else None), v
        q = _silu(q_ref[0, rows, :])
        ff = lb + (1.0 - lb) * _sigmoid(gf_ref[0, rows, :]) if need_f else None
        fb = lb + (1.0 - lb) * _sigmoid(gb_ref[0, rows, :]) if need_b else None
        return (q, (1.0 - ff if need_f else None), (1.0 - fb if need_b else None),
                (jnp.log(ff) if need_f else None), (jnp.log(fb) if need_b else None), v)

    if has_init:
        sf_scr[...] = s0_ref[0, 0, 0].T
        sb_scr[...] = s0_ref[0, 1, 0].T
    else:
        sf_scr[...] = jnp.zeros(sf_scr.shape, F32)
        sb_scr[...] = jnp.zeros(sb_scr.shape, F32)

    def fwd(n, carry):
        q, kf, kb, gf, gb, v = load(n, True, True)
        bf = _prefix_rows(gf, row)
        cb = _suffix_rows(gb, row)
        vb = v.astype(BF16)
        att = _pdim_intra(q, kf, kb, bf, cb, row, ii, jj)
        o = _dot(att.astype(BF16), vb)
        o = o + _dot_nt((q * jnp.exp(bf)).astype(BF16), sf_scr[...].astype(BF16))
        o_scr[pl.ds(pl.multiple_of(n * c, c), c), :] = o
        b_end = bf[c - 1:c, :]
        sf_scr[...] = (sf_scr[...] * jnp.exp(b_end)
                       + _dot_tn(vb, (kf * jnp.exp(b_end - bf)).astype(BF16)))
        return carry

    lax.fori_loop(0, nchunks, fwd, 0)

    def bwd(t, carry):
        n = nchunks - 1 - t
        q, _, kb, _, gb, v = load(n, False, True)
        cb = _suffix_rows(gb, row)
        rows = pl.ds(pl.multiple_of(n * c, c), c)
        o_scr[rows, :] += _dot_nt((q * jnp.exp(cb)).astype(BF16), sb_scr[...].astype(BF16))
        c_end = cb[0:1, :]
        sb_scr[...] = (sb_scr[...] * jnp.exp(c_end)
                       + _dot_tn(v.astype(BF16), (kb * jnp.exp(c_end - cb)).astype(BF16)))
        return carry

    lax.fori_loop(0, nchunks, bwd, 0)

    if want_state:
        sfin_ref[0, 0, 0] = sf_scr[...].T
        sfin_ref[0, 1, 0] = sb_scr[...].T

    ec = min(seq, SCALAR_CHUNK)
    for n in range(seq // ec):
        rows = slice(n * ec, (n + 1) * ec)
        o = o_scr[rows, :]
        if mode == "gla":
            o = o * lax.rsqrt(jnp.mean(o * o, axis=-1, keepdims=True) + EPS)
            o = o * nw_ref[...] * _silu(r_ref[0, rows, :])
        y_ref[0, rows, :] = o.astype(y_ref.dtype)


def _pdim_scan(mode, proj, gates, extra, *, b, boff, layer=0, s0=None, want_state):
    _, t, _ = proj.shape
    if mode == "gla":
        hh, dk, dv = GLA_HEADS, GLA_DK, GLA_DV
        in_specs = [pl.BlockSpec((1, t, dk), lambda i, h: (boff + i, 0, h)),
                    pl.BlockSpec((1, t, dk), lambda i, h: (boff + i, 0, hh + h)),
                    pl.BlockSpec((1, t, dv), lambda i, h: (boff + i, 0, 2 * hh * dk // dv + h)),
                    pl.BlockSpec((1, t, dk), lambda i, h: (boff + i, 0, h)),
                    pl.BlockSpec((1, t, dk), lambda i, h: (boff + i, 0, hh + h)),
                    pl.BlockSpec((1, t, dv), lambda i, h: (boff + i, 0, (2 * hh * dk + hh * dv) // dv + h)),
                    pl.BlockSpec((1, dv), lambda i, h: (0, 0))]
        args = [proj, proj, proj, gates, gates, proj, extra.reshape(1, dv)]
        out_dtype = BF16
    else:
        hh, dk, dv = HG_HEADS, HG_DF, HG_DI
        in_specs = [pl.BlockSpec((1, t, dk), lambda i, h: (boff + i, 0, h)),
                    pl.BlockSpec((1, t, dv), lambda i, h: (boff + i, 0, hh + h)),
                    pl.BlockSpec((1, t, dk), lambda i, h: (boff + i, 0, 3 * hh + h)),
                    pl.BlockSpec((1, t, dk), lambda i, h: (boff + i, 0, 4 * hh + h)),
                    pl.BlockSpec((DEPTH, dk), lambda i, h: (0, h))]
        args = [proj, proj, proj, proj, extra]
        out_dtype = F32
    st_spec = pl.BlockSpec((1, 2, 1, dk, dv), lambda i, h: (i, 0, h, 0, 0))
    if s0 is not None:
        in_specs.append(st_spec)
        args.append(s0)
    out_specs = [pl.BlockSpec((1, t, dv), lambda i, h: (i, 0, h))]
    out_shape = [jax.ShapeDtypeStruct((b, t, hh * dv), out_dtype)]
    if want_state:
        out_specs.append(st_spec)
        out_shape.append(jax.ShapeDtypeStruct((b, 2, hh, dk, dv), F32))
    res = pl.pallas_call(
        functools.partial(_pdim_kernel, seq=t, mode=mode, layer=layer, has_init=s0 is not None,
                          want_state=want_state),
        grid=(b, hh),
        in_specs=in_specs,
        out_specs=out_specs,
        out_shape=out_shape,
        scratch_shapes=[pltpu.VMEM((t, dv), F32), pltpu.VMEM((dv, dk), F32), pltpu.VMEM((dv, dk), F32)],
        compiler_params=_cparams(2),
        name=mode + "_scan",
    )(*args)
    return res if want_state else (res[0], None)


ROW_TILE = 1024
GATHER_ROWS = 256
N_MOD_ROWS = 16


def _full(n, v):
    return jnp.full((n,), v, jnp.int32)


def _both_groups(scan, proj, tok, seqs, **lat):
    (bc, tc), (bl, tl) = seqs
    rows, n = proj.shape
    yc = scan(proj.reshape(rows // tc, tc, n), b=bc, boff=0, want_state=True)
    yl = scan(proj.reshape(rows // tl, tl, n), b=bl, boff=tok.n_ctx // tl, want_state=False, **lat)
    y = jnp.concatenate([yc[0].reshape(bc * tc, -1), yl[0].reshape(bl * tl, -1)], axis=0)
    return y, yc[1:]


def kernel(x_prompt, x_sample, state_gla, state_mlstm_c, state_mlstm_n, state_mlstm_m, state_ret, state_hgrn, c, c_ctx, ada_w, ada_b, ln_g, ln_b, gla_w_in, gla_w_g1, gla_w_g2, gla_b_g, gla_norm_w, gla_w_out, ml_w_in, ml_w_gate, ml_b_gate, ml_norm_w, ml_w_out, ret_w_in, ret_decay, ret_norm_w, ret_w_out, hg_w_in, hg_w_f, hg_b_f, hg_lb, hg_norm_w, hg_w_out, ffn_w_gate, ffn_w_up, ffn_w_down, moe_router, moe_w_gate, moe_w_up, moe_w_down):
    bc, tc, d = x_prompt.shape
    bl, tl, _ = x_sample.shape
    n_ctx, n_lat = bc * tc, bl * tl
    m = n_ctx + n_lat
    tok = _Tokens(n_ctx, tl)
    tm = ROW_TILE
    assert d == D_MODEL and n_ctx % tm == 0 and tl % tm == 0 and 1 + bl <= N_MOD_ROWS and ada_w.shape[0] == DEPTH
    seqs = ((bc, tc), (bl, tl))
    x = jnp.concatenate([x_prompt.reshape(n_ctx, d), x_sample.reshape(n_lat, d)], axis=0)

    cvec = jnp.concatenate([c_ctx[None, :], c, jnp.zeros((N_MOD_ROWS - 1 - bl, d), F32)], axis=0)
    mods = [_linear(cvec, ada_w, widx=i, bias=ada_b[i], pro="silu", tm=N_MOD_ROWS).reshape(N_MOD_ROWS, 6, 1, d)
            for i in range(DEPTH)]
    n_tiles = m // tm

    def mix_out(y, w, i, rows=tm, hgrn=None):
        return _out_proj(y, w, _full(m // rows, 0), tk=min(y.shape[1], 1024), tm=rows,
                         resln=(x, mods[i], 2, ln_g[i, 0], ln_b[i, 0]), tok=tok, hgrn=hgrn)

    i = 0
    w_cat = jnp.concatenate([gla_w_in[0], gla_w_g1[0, 0], gla_w_g1[0, 1],
                             jnp.zeros((d, LANES - 2 * GLA_RANK), F32)], axis=1)
    proj = _linear(x, w_cat, mod=mods[i], mod_cols=(0, 1), tok=tok, pro="mod", tm=tm)
    nin = gla_w_in.shape[-1]
    half = GLA_HEADS * GLA_DK
    w2 = jnp.zeros((LANES, 2 * half), F32)
    w2 = w2.at[:GLA_RANK, :half].set(gla_w_g2[0, 0]).at[GLA_RANK:2 * GLA_RANK, half:].set(gla_w_g2[0, 1])
    gates = _linear(proj[:, nin:], w2, bias=gla_b_g[0].reshape(2 * half), epi="logsig_tau", tm=tm)
    scan = lambda p, **kw: _pdim_scan("gla", p, gates.reshape(p.shape[0], p.shape[1], 2 * half), gla_norm_w[0], **kw)
    y, (new_gla,) = _both_groups(scan, proj, tok, seqs, s0=state_gla[:, 0])
    x = mix_out(y, gla_w_out, i)
    u = _ffn_up(x, ffn_w_gate, ffn_w_up, _full(n_tiles, 0), mod=mods[i], tok=tok, tm=tm, tf=D_FF // 2)
    x = _out_proj(u, ffn_w_down, _full(n_tiles, 0), tk=D_FF // 2, tm=tm,
                  resln=(x, mods[i], 5, ln_g[i, 1], ln_b[i, 1]), tok=tok)

    i = 1
    wg0, wg1 = ml_w_gate[0, 0], ml_w_gate[0, 1]
    hh = ML_HEADS
    g4 = jnp.stack([wg0[:, :hh], wg0[:, hh:], wg1[:, :hh], wg1[:, hh:]], axis=-1)
    g4 = jnp.pad(g4, ((0, 0), (0, 0), (0, LANES - 4))).reshape(d, hh * LANES)
    b0, b1 = ml_b_gate[0, 0], ml_b_gate[0, 1]
    b4 = jnp.pad(jnp.stack([b0[:hh], b0[hh:], b1[:hh], b1[hh:]], axis=-1), ((0, 0), (0, LANES - 4)))
    w_cat = jnp.concatenate([ml_w_in[0], g4], axis=1)
    bias = jnp.concatenate([jnp.zeros((ml_w_in.shape[-1],), F32), b4.reshape(hh * LANES)])
    proj = _linear(x, w_cat, bias=bias, mod=mods[i], mod_cols=(0, 1), tok=tok, pro="mod", tm=tm)
    init = (state_mlstm_c[:, 0], state_mlstm_n[:, 0][:, :, :, None, :],
            jnp.broadcast_to(state_mlstm_m[:, 0][:, :, :, None, None], (bl, 2, hh, 1, LANES)))
    scan = lambda p, **kw: _mlstm_scan(p, ml_norm_w[0], **kw)
    y, (new_c, new_n, new_m) = _both_groups(scan, proj, tok, seqs, init=init)
    x = mix_out(y, ml_w_out, i)
    x = _moe_layer(x, mods[i], 0, moe_router, moe_w_gate, moe_w_up, moe_w_down, ln_g[i, 1], ln_b[i, 1], tok, tm)

    i = 2
    proj = _linear(x, ret_w_in, mod=mods[i], mod_cols=(0, 1), tok=tok, pro="mod", tm=tm)
    quarter = RET_DK // 4
    inv = jnp.power(ROPE_BASE, -jnp.arange(quarter, dtype=F32) / quarter)
    rws = jnp.repeat(jnp.arange(tl // GRID_W, dtype=F32), GRID_W)
    cls = jnp.tile(jnp.arange(GRID_W, dtype=F32), tl // GRID_W)
    ang = jnp.concatenate([rws[:, None] * inv, cls[:, None] * inv], axis=-1)
    scan = lambda p, **kw: _ret_scan(p, ret_decay[0], ret_norm_w[0], **kw)
    y, (new_ret,) = _both_groups(scan, proj, tok, seqs, s0=state_ret[:, 0],
                                 rope_tabs=(jnp.cos(ang), jnp.sin(ang)))
    x = mix_out(y, ret_w_out, i)
    u = _ffn_up(x, ffn_w_gate, ffn_w_up, _full(n_tiles, 1), mod=mods[i], tok=tok, tm=tm, tf=D_FF // 2)
    x = _out_proj(u, ffn_w_down, _full(n_tiles, 1), tk=D_FF // 2, tm=tm,
                  resln=(x, mods[i], 5, ln_g[i, 1], ln_b[i, 1]), tok=tok)

    i = 3
    w_cat = jnp.concatenate([hg_w_in[0], hg_w_f[0, 0], hg_w_f[0, 1]], axis=1)
    bias = jnp.concatenate([jnp.zeros((hg_w_in.shape[-1],), F32), hg_b_f[0, 0], hg_b_f[0, 1]])
    proj = _linear(x, w_cat, bias=bias, mod=mods[i], mod_cols=(0, 1), tok=tok, pro="mod", tm=tm)
    scan = lambda p, **kw: _pdim_scan("hgrn", p, None, hg_lb, layer=i, **kw)
    y, (new_hg,) = _both_groups(scan, proj, tok, seqs, s0=state_hgrn[:, 0])
    x = mix_out(y, hg_w_out, i, rows=tm // 2, hgrn=(proj, 2, hg_norm_w[0]))
    x = _moe_layer(x, mods[i], 1, moe_router, moe_w_gate, moe_w_up, moe_w_down, ln_g[i, 1], ln_b[i, 1], tok, tm)

    y_prompt = x[:n_ctx].reshape(bc, tc, d)
    y_sample = x[n_ctx:].reshape(bl, tl, d)
    return (y_prompt, y_sample, new_gla[:, None], new_c[:, None], new_n[:, None, :, :, 0, :],
            new_m[:, None, :, :, 0, 0], new_ret[:, None], new_hg[:, None])


def _moe_layer(x, mod, j, router, w_gate, w_up, w_down, lg, lb, tok, tm):
    m, d = x.shape
    ne = N_EXPERTS
    wr = jnp.pad(router, ((0, 0), (0, 0), (0, LANES - ne)))
    h, route = _router(x, mod, wr, widx=j, tok=tok, tm=tm)
    eid = jnp.concatenate([route[:, 0], route[:, 1]]).astype(jnp.int32)
    onehot = (eid[:, None] == jnp.arange(ne, dtype=jnp.int32)[None, :]).astype(jnp.int32)
    csum = jnp.cumsum(onehot, axis=0)
    rank = jnp.sum(onehot * csum, axis=1) - 1
    counts = csum[-1]
    padded = ((counts + tm - 1) // tm) * tm
    ends = jnp.cumsum(padded)
    pos = (ends - padded)[eid] + rank
    m_pad = 2 * m + ne * tm
    src = jnp.zeros((m_pad,), jnp.int32).at[pos].set(jnp.tile(jnp.arange(m, dtype=jnp.int32), 2))
    starts = jnp.arange(m_pad // tm, dtype=jnp.int32) * tm
    tile_e = jnp.minimum(jnp.sum((starts[:, None] >= ends[None, :]).astype(jnp.int32), axis=1), ne - 1) + ne * j
    xs = _gather_rows(h, src, rows=GATHER_ROWS)
    f = w_gate.shape[-1]
    us = _ffn_up(xs, w_gate.reshape(-1, d, f), w_up.reshape(-1, d, f), tile_e, tm=tm, tf=f // 2)
    ys = _out_proj(us, w_down.reshape(-1, f, d), tile_e, tk=f // 2, tm=tm)
    return _moe_combine(ys, route, pos[:m], pos[m:], x, mod, lg, lb, tok=tok, rows=GATHER_ROWS)
```

```python
import functools

import jax
import jax.numpy as jnp
from jax import lax
from jax.experimental import pallas as pl
from jax.experimental.pallas import tpu as pltpu

F32 = jnp.float32
BF16 = jnp.bfloat16
HIGHEST = lax.Precision.HIGHEST

D_MODEL = 1024
DEPTH = 4
GRID_W = 64
ALPHA = (2.0 * DEPTH) ** 0.25
EPS = 1e-5
ROPE_BASE = 10000.0
GLA_HEADS, GLA_DK, GLA_DV, GLA_RANK, GLA_TAU = 4, 128, 256, 16, 16.0
ML_HEADS, ML_D = 4, 256
RET_HEADS, RET_DK, RET_DV = 4, 256, 512
HG_HEADS, HG_DF, HG_DI = 8, 128, 128
D_FF = 2816
N_EXPERTS = 8

V7X_VMEM_LIMIT_BYTES = 56 * 1024 * 1024
LANES = 128
SUBLANES = 8

SCALAR_CHUNK = 256
PDIM_CHUNK = 64


def _cparams(n_axes):
    return pltpu.CompilerParams(dimension_semantics=("arbitrary",) * n_axes,
                                vmem_limit_bytes=V7X_VMEM_LIMIT_BYTES)


def _sigmoid(x):
    return jax.nn.sigmoid(x)


def _silu(x):
    return x * jax.nn.sigmoid(x)


def _log_sigmoid(x):
    return jnp.minimum(x, 0.0) - jnp.log1p(jnp.exp(-jnp.abs(x)))


def _dot(a, b):
    return jnp.dot(a, b, preferred_element_type=F32)


def _dot_nt(a, b):
    return lax.dot_general(a, b, (((1,), (1,)), ((), ())), preferred_element_type=F32)


def _dot_tn(a, b):
    return lax.dot_general(a, b, (((0,), (0,)), ((), ())), preferred_element_type=F32)


def _largest_tile(n, cap):
    best = None
    for t in range(LANES, min(n, cap) + 1, LANES):
        if n % t == 0:
            best = t
    assert best is not None, (n, cap)
    return best


class _Tokens:
    def __init__(self, n_ctx, lat_seq):
        self.n_ctx, self.lat_seq = n_ctx, lat_seq

    def group(self, i, tm):
        r = i * tm
        return jnp.where(r < self.n_ctx, 0, 1 + (r - self.n_ctx) // self.lat_seq)


def _linear_kernel(*refs, pro, epi, has_bias):
    it = iter(refs)
    x_ref = next(it)
    sh_ref = sc_ref = b_ref = None
    if pro == "mod":
        sh_ref, sc_ref = next(it), next(it)
    w_ref = next(it)
    if has_bias:
        b_ref = next(it)
    o_ref = next(it)
    h_scr = next(it)

    @pl.when(pl.program_id(1) == 0)
    def _():
        x = x_ref[...].astype(F32)
        if pro == "mod":
            x = x * (1.0 + sc_ref[...]) + sh_ref[...]
        elif pro == "silu":
            x = _silu(x)
        h_scr[...] = x.astype(BF16)

    acc = _dot(h_scr[...], w_ref[...].astype(BF16))
    if has_bias:
        acc = acc + b_ref[...]
    if epi == "logsig_tau":
        acc = _log_sigmoid(acc) / GLA_TAU
    o_ref[...] = acc.astype(o_ref.dtype)


def _linear(x, w, *, widx=0, bias=None, mod=None, mod_cols=None, tok=None, pro=None, epi=None,
            tm, tn_cap=1536, out_dtype=F32):
    m, k = x.shape
    w3 = w if w.ndim == 3 else w.reshape(1, *w.shape)
    n = w3.shape[-1]
    tn = _largest_tile(n, tn_cap)
    assert m % tm == 0
    in_specs = [pl.BlockSpec((tm, k), lambda i, j: (i, 0))]
    args = [x]
    if pro == "mod":
        for c in mod_cols:
            in_specs.append(pl.BlockSpec((None, None, 1, k),
                                         lambda i, j, c=c: (tok.group(i, tm), c, 0, 0)))
            args.append(mod)
    in_specs.append(pl.BlockSpec((None, k, tn), lambda i, j: (widx, 0, j)))
    args.append(w3)
    if bias is not None:
        in_specs.append(pl.BlockSpec((1, tn), lambda i, j: (0, j)))
        args.append(bias.reshape(1, n).astype(F32))
    return pl.pallas_call(
        functools.partial(_linear_kernel, pro=pro, epi=epi, has_bias=bias is not None),
        grid=(m // tm, n // tn),
        in_specs=in_specs,
        out_specs=pl.BlockSpec((tm, tn), lambda i, j: (i, j)),
        out_shape=jax.ShapeDtypeStruct((m, n), out_dtype),
        scratch_shapes=[pltpu.VMEM((tm, k), BF16)],
        compiler_params=_cparams(2),
        name="linear",
    )(*args)


FFN_COL_CHUNK = 256


def _ffn_up_kernel(e_ref, *refs, pro):
    it = iter(refs)
    x_ref = next(it)
    sh_ref = sc_ref = None
    if pro == "mod":
        sh_ref, sc_ref = next(it), next(it)
    wg_ref, wu_ref, o_ref, h_scr = next(it), next(it), next(it), next(it)

    @pl.when(pl.program_id(1) == 0)
    def _():
        x = x_ref[...].astype(F32)
        if pro == "mod":
            x = x * (1.0 + sc_ref[...]) + sh_ref[...]
        h_scr[...] = x.astype(BF16)

    h = h_scr[...]
    tf = o_ref.shape[-1]
    for c0 in range(0, tf, FFN_COL_CHUNK):
        c1 = min(c0 + FFN_COL_CHUNK, tf)
        g = _dot(h, wg_ref[:, c0:c1].astype(BF16))
        u = _dot(h, wu_ref[:, c0:c1].astype(BF16))
        o_ref[:, c0:c1] = (_silu(g) * u).astype(o_ref.dtype)


def _ffn_up(x, wg, wu, tile_expert, *, mod=None, tok=None, tm, tf):
    m, k = x.shape
    f = wg.shape[-1]
    pro = "mod" if mod is not None else None
    in_specs = [pl.BlockSpec((tm, k), lambda i, j, e: (i, 0))]
    args = [x]
    if pro == "mod":
        for c in (3, 4):
            in_specs.append(pl.BlockSpec((None, None, 1, k),
                                         lambda i, j, e, c=c: (tok.group(i, tm), c, 0, 0)))
            args.append(mod)
    wspec = pl.BlockSpec((None, k, tf), lambda i, j, e: (e[i], 0, j))
    in_specs += [wspec, wspec]
    args += [wg, wu]
    return pl.pallas_call(
        functools.partial(_ffn_up_kernel, pro=pro),
        grid_spec=pltpu.PrefetchScalarGridSpec(
            num_scalar_prefetch=1,
            grid=(m // tm, f // tf),
            in_specs=in_specs,
            out_specs=pl.BlockSpec((tm, tf), lambda i, j, e: (i, j)),
            scratch_shapes=[pltpu.VMEM((tm, k), BF16)],
        ),
        out_shape=jax.ShapeDtypeStruct((m, f), BF16),
        compiler_params=_cparams(2),
        name="ffn_up",
    )(tile_expert, *args)


def _layer_norm_rows(x, g, b):
    x = x - jnp.mean(x, axis=-1, keepdims=True)
    x = x * lax.rsqrt(jnp.mean(x * x, axis=-1, keepdims=True) + EPS)
    return x * g + b


def _out_kernel(e_ref, *refs, pro, epi, nk):
    it = iter(refs)
    y_ref = next(it)
    gin_ref = nw_ref = x_ref = gate_ref = lg_ref = lb_ref = None
    if pro == "hgrn":
        gin_ref, nw_ref = next(it), next(it)
    w_ref = next(it)
    if epi == "resln":
        x_ref, gate_ref, lg_ref, lb_ref = next(it), next(it), next(it), next(it)
    o_ref = next(it)
    kk = pl.program_id(1)

    y = y_ref[...]
    if pro == "hgrn":
        y = y.astype(F32)
        y = y * lax.rsqrt(jnp.mean(y * y, axis=-1, keepdims=True) + EPS)
        y = y * nw_ref[...] * _silu(gin_ref[...])
    part = _dot(y.astype(BF16), w_ref[...].astype(BF16))

    if nk > 1:
        @pl.when(kk == 0)
        def _():
            o_ref[...] = part

        @pl.when(kk > 0)
        def _():
            o_ref[...] += part

    def finish():
        acc = part if nk == 1 else o_ref[...]
        if epi == "resln":
            acc = _layer_norm_rows(ALPHA * x_ref[...] + gate_ref[...] * acc, lg_ref[...], lb_ref[...])
        o_ref[...] = acc

    if nk == 1:
        finish()
    elif epi == "resln":
        pl.when(kk == nk - 1)(finish)


def _out_proj(y, w, tile_expert, *, tk, tm, resln=None, hgrn=None, tok=None):
    m, k = y.shape
    d = w.shape[-1]
    nk = k // tk
    assert k % tk == 0 and (hgrn is None or nk == 1)
    in_specs = [pl.BlockSpec((tm, tk), lambda i, kk, e: (i, kk))]
    args = [y]
    if hgrn is not None:
        proj, gblk, nw = hgrn
        in_specs += [pl.BlockSpec((tm, k), lambda i, kk, e: (i, gblk)),
                     pl.BlockSpec((1, k), lambda i, kk, e: (0, 0))]
        args += [proj, nw.reshape(1, k)]
    in_specs.append(pl.BlockSpec((None, tk, d), lambda i, kk, e: (e[i], kk, 0)))
    args.append(w)
    if resln is not None:
        x, mod, gcol, lg, lb = resln
        in_specs += [pl.BlockSpec((tm, d), lambda i, kk, e: (i, 0)),
                     pl.BlockSpec((None, None, 1, d), lambda i, kk, e: (tok.group(i, tm), gcol, 0, 0)),
                     pl.BlockSpec((1, d), lambda i, kk, e: (0, 0)),
                     pl.BlockSpec((1, d), lambda i, kk, e: (0, 0))]
        args += [x, mod, lg.reshape(1, d), lb.reshape(1, d)]
    return pl.pallas_call(
        functools.partial(_out_kernel, pro="hgrn" if hgrn is not None else None,
                          epi="resln" if resln is not None else None, nk=nk),
        grid_spec=pltpu.PrefetchScalarGridSpec(
            num_scalar_prefetch=1,
            grid=(m // tm, nk),
            in_specs=in_specs,
            out_specs=pl.BlockSpec((tm, d), lambda i, kk, e: (i, 0)),
        ),
        out_shape=jax.ShapeDtypeStruct((m, d), F32),
        compiler_params=_cparams(2),
        name="out_proj",
    )(tile_expert, *args)


def _router_kernel(x_ref, sh_ref, sc_ref, wr_ref, r_ref):
    h = x_ref[...] * (1.0 + sc_ref[...]) + sh_ref[...]
    logits = _dot(h.astype(BF16), wr_ref[...].astype(BF16))
    lane = lax.broadcasted_iota(jnp.int32, logits.shape, 1)
    neg = jnp.float32(-jnp.inf)
    logits = jnp.where(lane < N_EXPERTS, logits, neg)
    m1 = jnp.max(logits, axis=-1, keepdims=True)
    i1 = jnp.min(jnp.where(logits == m1, lane, LANES), axis=-1, keepdims=True)
    rest = jnp.where(lane == i1, neg, logits)
    m2 = jnp.max(rest, axis=-1, keepdims=True)
    i2 = jnp.min(jnp.where(rest == m2, lane, LANES), axis=-1, keepdims=True)
    e2 = jnp.exp(m2 - m1)
    w1 = 1.0 / (1.0 + e2)
    w2 = e2 / (1.0 + e2)
    out = jnp.where(lane == 0, i1.astype(F32), 0.0)
    out = jnp.where(lane == 1, i2.astype(F32), out)
    out = jnp.where(lane == 2, w1, out)
    out = jnp.where(lane == 3, w2, out)
    r_ref[...] = out


def _router(x, mod, wr_pad, *, widx, tok, tm):
    m, k = x.shape
    return pl.pallas_call(
        _router_kernel,
        grid=(m // tm,),
        in_specs=[pl.BlockSpec((tm, k), lambda i: (i, 0)),
                  pl.BlockSpec((None, None, 1, k), lambda i: (tok.group(i, tm), 3, 0, 0)),
                  pl.BlockSpec((None, None, 1, k), lambda i: (tok.group(i, tm), 4, 0, 0)),
                  pl.BlockSpec((None, k, LANES), lambda i: (widx, 0, 0))],
        out_specs=pl.BlockSpec((tm, LANES), lambda i: (i, 0)),
        out_shape=jax.ShapeDtypeStruct((m, LANES), F32),
        compiler_params=_cparams(1),
        name="router",
    )(x, mod, mod, wr_pad)


DMA_ISSUE_UNROLL = 8


def _row_copy(src_ref, dst_ref, sem, src_row, dst_row):
    return pltpu.make_async_copy(src_ref.at[pl.ds(src_row, 1), :], dst_ref.at[pl.ds(dst_row, 1), :], sem)


def _dispatch_kernel(p1_ref, p2_ref, x_ref, sh_ref, sc_ref, init_ref, xs_ref, h_scr, sem, *, rows):
    del init_ref
    base = pl.program_id(0) * rows
    h_scr[...] = x_ref[...] * (1.0 + sc_ref[...]) + sh_ref[...]

    def start(r, c):
        _row_copy(h_scr, xs_ref, sem, r, p1_ref[base + r]).start()
        _row_copy(h_scr, xs_ref, sem, r, p2_ref[base + r]).start()
        return c

    lax.fori_loop(0, rows, start, 0, unroll=DMA_ISSUE_UNROLL)
    pltpu.make_async_copy(xs_ref.at[pl.ds(0, 2 * rows), :], xs_ref.at[pl.ds(0, 2 * rows), :], sem).wait()


def _dispatch(x, mod, p1, p2, n_out, *, tok, rows):
    m, d = x.shape
    return pl.pallas_call(
        functools.partial(_dispatch_kernel, rows=rows),
        grid_spec=pltpu.PrefetchScalarGridSpec(
            num_scalar_prefetch=2,
            grid=(m // rows,),
            in_specs=[pl.BlockSpec((rows, d), lambda i, p1, p2: (i, 0)),
                      pl.BlockSpec((None, None, 1, d), lambda i, p1, p2: (tok.group(i, rows), 3, 0, 0)),
                      pl.BlockSpec((None, None, 1, d), lambda i, p1, p2: (tok.group(i, rows), 4, 0, 0)),
                      pl.BlockSpec(memory_space=pl.ANY)],
            out_specs=pl.BlockSpec(memory_space=pl.ANY),
            scratch_shapes=[pltpu.VMEM((rows, d), F32), pltpu.SemaphoreType.DMA(())],
        ),
        out_shape=jax.ShapeDtypeStruct((n_out, d), F32),
        input_output_aliases={5: 0},
        compiler_params=_cparams(1),
        name="moe_dispatch",
    )(p1, p2, x, mod, mod, jnp.zeros((n_out, d), F32))


def _combine_kernel(p1_ref, p2_ref, ys_ref, r_ref, x_ref, gate_ref, lg_ref, lb_ref, o_ref,
                    a_scr, b_scr, sem, *, rows):
    base = pl.program_id(0) * rows

    def start(r, c):
        _row_copy(ys_ref, a_scr, sem.at[0], p1_ref[base + r], r).start()
        _row_copy(ys_ref, b_scr, sem.at[1], p2_ref[base + r], r).start()
        return c

    lax.fori_loop(0, rows, start, 0, unroll=DMA_ISSUE_UNROLL)
    pltpu.make_async_copy(ys_ref.at[pl.ds(0, rows), :], a_scr, sem.at[0]).wait()
    pltpu.make_async_copy(ys_ref.at[pl.ds(0, rows), :], b_scr, sem.at[1]).wait()
    rt = r_ref[...]
    f = rt[:, 2:3] * a_scr[...] + rt[:, 3:4] * b_scr[...]
    o_ref[...] = _layer_norm_rows(ALPHA * x_ref[...] + gate_ref[...] * f, lg_ref[...], lb_ref[...])


def _moe_combine(ys, route, p1, p2, x, mod, lg, lb, *, tok, rows):
    m, d = x.shape
    return pl.pallas_call(
        functools.partial(_combine_kernel, rows=rows),
        grid_spec=pltpu.PrefetchScalarGridSpec(
            num_scalar_prefetch=2,
            grid=(m // rows,),
            in_specs=[pl.BlockSpec(memory_space=pl.ANY),
                      pl.BlockSpec((rows, LANES), lambda i, p1, p2: (i, 0)),
                      pl.BlockSpec((rows, d), lambda i, p1, p2: (i, 0)),
                      pl.BlockSpec((None, None, 1, d), lambda i, p1, p2: (tok.group(i, rows), 5, 0, 0)),
                      pl.BlockSpec((1, d), lambda i, p1, p2: (0, 0)),
                      pl.BlockSpec((1, d), lambda i, p1, p2: (0, 0))],
            out_specs=pl.BlockSpec((rows, d), lambda i, p1, p2: (i, 0)),
            scratch_shapes=[pltpu.VMEM((rows, d), F32), pltpu.VMEM((rows, d), F32),
                            pltpu.SemaphoreType.DMA((2,))],
        ),
        out_shape=jax.ShapeDtypeStruct((m, d), F32),
        compiler_params=_cparams(1),
        name="moe_combine",
    )(p1, p2, ys, route, x, mod, lg.reshape(1, d), lb.reshape(1, d))


def _ret_kernel(dec_ref, *refs, seq, has_init, want_state, rope):
    it = iter(refs)
    q_ref, k_ref, v_ref, g_ref, nw_ref = (next(it) for _ in range(5))
    cos_ref = sin_ref = s0_ref = sfin_ref = None
    if rope:
        cos_ref, sin_ref = next(it), next(it)
    if has_init:
        s0_ref = next(it)
    next(it)
    y_ref = next(it)
    if want_state:
        sfin_ref = next(it)
    q_scr, k_scr, o_scr, sf_scr, sb_scr = (next(it) for _ in range(5))

    c = SCALAR_CHUNK
    nchunks = seq // c
    h = pl.program_id(1)
    lgf = _log_sigmoid(jnp.full((1, 1), dec_ref[h], F32))
    lgb = _log_sigmoid(jnp.full((1, 1), dec_ref[RET_HEADS + h], F32))

    ii = lax.broadcasted_iota(jnp.int32, (c, c), 0)
    jj = lax.broadcasted_iota(jnp.int32, (c, c), 1)
    dlt = (ii - jj).astype(F32)
    decay = (jnp.where(dlt >= 0, jnp.exp(lgf * jnp.maximum(dlt, 0.0)), 0.0)
             + jnp.where(dlt <= 0, jnp.exp(lgb * jnp.maximum(-dlt, 0.0)), 0.0))
    pos = lax.broadcasted_iota(jnp.int32, (c, 1), 0).astype(F32)
    q_in_f = jnp.exp(lgf * (pos + 1.0))
    q_in_b = jnp.exp(lgb * (c - pos))
    k_out_f = jnp.exp(lgf * (c - 1.0 - pos))
    k_out_b = jnp.exp(lgb * pos)
    tot_f = jnp.exp(lgf * c)
    tot_b = jnp.exp(lgb * c)

    half = RET_DK // 2
    for n in range(nchunks):
        rows = slice(n * c, (n + 1) * c)
        q = q_ref[0, rows, :]
        k = k_ref[0, rows, :] * (RET_DK ** -0.5)
        if rope:
            cs, sn = cos_ref[rows, :], sin_ref[rows, :]
            q = jnp.concatenate([q[:, :half] * cs - q[:, half:] * sn, q[:, :half] * sn + q[:, half:] * cs], axis=1)
            k = jnp.concatenate([k[:, :half] * cs - k[:, half:] * sn, k[:, :half] * sn + k[:, half:] * cs], axis=1)
        q_scr[rows, :] = q
        k_scr[rows, :] = k

    if has_init:
        sf_scr[...] = s0_ref[0, 0, 0]
        sb_scr[...] = s0_ref[0, 1, 0]

    for n in range(nchunks):
        rows = slice(n * c, (n + 1) * c)
        q, k = q_scr[rows, :], k_scr[rows, :]
        vb = v_ref[0, rows, :].astype(BF16)
        att = _dot_nt(q.astype(BF16), k.astype(BF16)) * decay
        o = _dot(att.astype(BF16), vb)
        live = has_init or n > 0
        if live:
            o = o + _dot((q * q_in_f).astype(BF16), sf_scr[...].astype(BF16))
        o_scr[rows, :] = o
        if n < nchunks - 1 or want_state:
            upd = _dot_tn((k * k_out_f).astype(BF16), vb)
            sf_scr[...] = tot_f * sf_scr[...] + upd if live else upd

    for n in reversed(range(nchunks)):
        rows = slice(n * c, (n + 1) * c)
        q, k = q_scr[rows, :], k_scr[rows, :]
        live = has_init or n < nchunks - 1
        if live:
            o_scr[rows, :] += _dot((q * q_in_b).astype(BF16), sb_scr[...].astype(BF16))
        if n > 0 or want_state:
            upd = _dot_tn((k * k_out_b).astype(BF16), v_ref[0, rows, :].astype(BF16))
            sb_scr[...] = tot_b * sb_scr[...] + upd if live else upd

    if want_state:
        sfin_ref[0, 0, 0] = sf_scr[...]
        sfin_ref[0, 1, 0] = sb_scr[...]

    for n in range(nchunks):
        rows = slice(n * c, (n + 1) * c)
        o = o_scr[rows, :]
        o = o - jnp.mean(o, axis=-1, keepdims=True)
        o = o * lax.rsqrt(jnp.mean(o * o, axis=-1, keepdims=True) + EPS)
        y_ref[0, rows, :] = (o * nw_ref[...] * _silu(g_ref[0, rows, :])).astype(y_ref.dtype)


def _y_buffer(y_init, proj, width, dtype):
    shape = (proj.shape[0], proj.shape[1], width)
    return jnp.zeros(shape, dtype) if y_init is None else y_init.reshape(shape)


def _ret_scan(proj, decay, norm_w, *, b, boff, s0=None, want_state, rope_tabs=None, y_init=None):
    _, t, _ = proj.shape
    hh, dk, dv = RET_HEADS, RET_DK, RET_DV
    koff, voff, goff = hh * dk // dk, 2 * hh * dk // dv, (2 * hh * dk + hh * dv) // dv
    in_specs = [pl.BlockSpec((1, t, dk), lambda i, h, d: (boff + i, 0, h)),
                pl.BlockSpec((1, t, dk), lambda i, h, d: (boff + i, 0, koff + h)),
                pl.BlockSpec((1, t, dv), lambda i, h, d: (boff + i, 0, voff + h)),
                pl.BlockSpec((1, t, dv), lambda i, h, d: (boff + i, 0, goff + h)),
                pl.BlockSpec((1, dv), lambda i, h, d: (0, h))]
    args = [proj, proj, proj, proj, norm_w.reshape(1, hh * dv)]
    if rope_tabs is not None:
        in_specs += [pl.BlockSpec((t, dk // 2), lambda i, h, d: (0, 0))] * 2
        args += list(rope_tabs)
    if s0 is not None:
        in_specs.append(pl.BlockSpec((1, 2, 1, dk, dv), lambda i, h, d: (i, 0, h, 0, 0)))
        args.append(s0)
    y_init = _y_buffer(y_init, proj, hh * dv, BF16)
    in_specs.append(pl.BlockSpec(memory_space=pl.ANY))
    args.append(y_init)
    out_specs = [pl.BlockSpec((1, t, dv), lambda i, h, d: (boff + i, 0, h))]
    out_shape = [jax.ShapeDtypeStruct(y_init.shape, y_init.dtype)]
    if want_state:
        out_specs.append(pl.BlockSpec((1, 2, 1, dk, dv), lambda i, h, d: (i, 0, h, 0, 0)))
        out_shape.append(jax.ShapeDtypeStruct((b, 2, hh, dk, dv), F32))
    res = pl.pallas_call(
        functools.partial(_ret_kernel, seq=t, has_init=s0 is not None, want_state=want_state,
                          rope=rope_tabs is not None),
        grid_spec=pltpu.PrefetchScalarGridSpec(
            num_scalar_prefetch=1,
            grid=(b, hh),
            in_specs=in_specs,
            out_specs=out_specs,
            scratch_shapes=[pltpu.VMEM((t, dk), F32), pltpu.VMEM((t, dk), F32), pltpu.VMEM((t, dv), F32),
                            pltpu.VMEM((dk, dv), F32), pltpu.VMEM((dk, dv), F32)],
        ),
        out_shape=out_shape,
        input_output_aliases={len(args): 0},
        compiler_params=_cparams(2),
        name="ret_scan",
    )(decay.reshape(2 * hh).astype(F32), *args)
    return res if want_state else (res[0], None)


def _mlstm_dir(q, qb, k, vb, s, cum_col, cum_row, ib_col, ib_row, mask, edge, c_ref, n_ref, m_ref,
               update_state):
    m_prev = m_ref[...]
    logw = jnp.where(mask, cum_col + (ib_row - cum_row), -jnp.inf)
    log_prev = cum_col + m_prev
    m_t = jnp.maximum(log_prev, jnp.max(logw, axis=1, keepdims=True))
    w_prev = jnp.exp(log_prev - m_t)
    qk = s * jnp.exp(logw - m_t)
    num = w_prev * _dot(qb, c_ref[...].astype(BF16)) + _dot(qk.astype(BF16), vb)
    den = w_prev * jnp.sum(q * n_ref[...], axis=1, keepdims=True) + jnp.sum(qk, axis=1, keepdims=True)
    out = num / jnp.maximum(jnp.abs(den), jnp.exp(-m_t))
    if update_state:
        m_new = m_t[edge:edge + 1, :]
        cum_edge = cum_col[edge:edge + 1, :]
        w_end = jnp.exp(cum_edge - cum_col + ib_col - m_new)
        dec = jnp.exp(cum_edge + m_prev - m_new)
        kw = k * w_end
        c_ref[...] = dec * c_ref[...] + _dot_tn(kw.astype(BF16), vb)
        n_ref[...] = dec * n_ref[...] + jnp.sum(kw, axis=0, keepdims=True)
        m_ref[...] = m_new
    return out


def _mlstm_kernel(*refs, seq, has_init, want_state):
    it = iter(refs)
    q_ref, k_ref, v_ref, og_ref, z_ref, nw_ref = (next(it) for _ in range(6))
    c0_ref = n0_ref = m0_ref = cf_ref = nf_ref = mf_ref = None
    if has_init:
        c0_ref, n0_ref, m0_ref = next(it), next(it), next(it)
    next(it)
    y_ref = next(it)
    if want_state:
        cf_ref, nf_ref, mf_ref = next(it), next(it), next(it)
    o_scr, c_scr, n_scr, m_scr = (next(it) for _ in range(4))

    c = SCALAR_CHUNK
    nchunks = seq // c
    ii = lax.broadcasted_iota(jnp.int32, (c, c), 0)
    jj = lax.broadcasted_iota(jnp.int32, (c, c), 1)
    lower = ii >= jj
    upper = ii <= jj
    tri_lo = lower.astype(F32)
    tri_up = upper.astype(F32)

    for d in range(2):
        if has_init:
            c_scr[d] = c0_ref[0, d, 0]
            n_scr[d] = n0_ref[0, d, 0]
            m_scr[d] = m0_ref[0, d, 0][:, :1]
        else:
            c_scr[d] = jnp.zeros(c_scr.shape[1:], F32)
            n_scr[d] = jnp.zeros(n_scr.shape[1:], F32)
            m_scr[d] = jnp.zeros(m_scr.shape[1:], F32)

    def chunk(n, d):
        rows = slice(n * c, (n + 1) * c)
        q = q_ref[0, rows, :]
        k = k_ref[0, rows, :] * (ML_D ** -0.5)
        qb, kb, vb = q.astype(BF16), k.astype(BF16), v_ref[0, rows, :].astype(BF16)
        s = _dot_nt(qb, kb)
        z = z_ref[0, rows, :]
        zt = z.T
        lf, lft = _log_sigmoid(z), _log_sigmoid(zt)
        last = (n == nchunks - 1) if d == 0 else (n == 0)
        upd = (not last) or want_state
        if d == 0:
            cum_col = jnp.dot(tri_lo, lf, precision=HIGHEST, preferred_element_type=F32)[:, 1:2]
            cum_row = jnp.dot(lft, tri_up, precision=HIGHEST, preferred_element_type=F32)[1:2, :]
            return _mlstm_dir(q, qb, k, vb, s, cum_col, cum_row, z[:, 0:1], zt[0:1, :], lower, c - 1,
                              c_scr.at[0], n_scr.at[0], m_scr.at[0], upd)
        cum_col = jnp.dot(tri_up, lf, precision=HIGHEST, preferred_element_type=F32)[:, 3:4]
        cum_row = jnp.dot(lft, tri_lo, precision=HIGHEST, preferred_element_type=F32)[3:4, :]
        return _mlstm_dir(q, qb, k, vb, s, cum_col, cum_row, z[:, 2:3], zt[2:3, :], upper, 0,
                          c_scr.at[1], n_scr.at[1], m_scr.at[1], upd)

    for n in range(nchunks):
        o_scr[n * c:(n + 1) * c, :] = chunk(n, 0)
    for n in reversed(range(nchunks)):
        o_scr[n * c:(n + 1) * c, :] += chunk(n, 1)

    if want_state:
        for d in range(2):
            cf_ref[0, d, 0] = c_scr[d]
            nf_ref[0, d, 0] = n_scr[d]
            mf_ref[0, d, 0] = jnp.broadcast_to(m_scr[d], (1, LANES))

    for n in range(nchunks):
        rows = slice(n * c, (n + 1) * c)
        o = o_scr[rows, :]
        o = o - jnp.mean(o, axis=-1, keepdims=True)
        o = o * lax.rsqrt(jnp.mean(o * o, axis=-1, keepdims=True) + EPS)
        y_ref[0, rows, :] = (o * nw_ref[...] * _sigmoid(og_ref[0, rows, :])).astype(y_ref.dtype)


def _mlstm_scan(proj, norm_w, *, b, boff, init=None, want_state, y_init=None):
    _, t, _ = proj.shape
    hh, d = ML_HEADS, ML_D
    zoff = 4 * hh * d // LANES
    in_specs = [pl.BlockSpec((1, t, d), lambda i, h: (boff + i, 0, h)),
                pl.BlockSpec((1, t, d), lambda i, h: (boff + i, 0, hh + h)),
                pl.BlockSpec((1, t, d), lambda i, h: (boff + i, 0, 2 * hh + h)),
                pl.BlockSpec((1, t, d), lambda i, h: (boff + i, 0, 3 * hh + h)),
                pl.BlockSpec((1, t, LANES), lambda i, h: (boff + i, 0, zoff + h)),
                pl.BlockSpec((1, d), lambda i, h: (0, h))]
    args = [proj] * 5 + [norm_w.reshape(1, hh * d)]
    st_specs = [pl.BlockSpec((1, 2, 1, d, d), lambda i, h: (i, 0, h, 0, 0)),
                pl.BlockSpec((1, 2, 1, 1, d), lambda i, h: (i, 0, h, 0, 0)),
                pl.BlockSpec((1, 2, 1, 1, LANES), lambda i, h: (i, 0, h, 0, 0))]
    if init is not None:
        in_specs += st_specs
        args += list(init)
    y_init = _y_buffer(y_init, proj, hh * d, BF16)
    in_specs.append(pl.BlockSpec(memory_space=pl.ANY))
    args.append(y_init)
    out_specs = [pl.BlockSpec((1, t, d), lambda i, h: (boff + i, 0, h))]
    out_shape = [jax.ShapeDtypeStruct(y_init.shape, y_init.dtype)]
    if want_state:
        out_specs += st_specs
        out_shape += [jax.ShapeDtypeStruct((b, 2, hh, d, d), F32),
                      jax.ShapeDtypeStruct((b, 2, hh, 1, d), F32),
                      jax.ShapeDtypeStruct((b, 2, hh, 1, LANES), F32)]
    res = pl.pallas_call(
        functools.partial(_mlstm_kernel, seq=t, has_init=init is not None, want_state=want_state),
        grid=(b, hh),
        in_specs=in_specs,
        out_specs=out_specs,
        out_shape=out_shape,
        scratch_shapes=[pltpu.VMEM((t, d), F32), pltpu.VMEM((2, d, d), F32), pltpu.VMEM((2, 1, d), F32),
                        pltpu.VMEM((2, 1, 1), F32)],
        input_output_aliases={len(args) - 1: 0},
        compiler_params=_cparams(2),
        name="mlstm_scan",
    )(*args)
    return res


def _group_bcast(x, row, group, r):
    n, d = x.shape
    if group >= SUBLANES:
        picked = x.reshape(n // group, group, d)[:, r:r + 1, :]
        return jnp.broadcast_to(picked, (n // group, group, d)).reshape(n, d)
    pos = row & (group - 1)
    out = x
    for off in range(-r, group - r):
        if off != 0:
            out = jnp.where(pos - r == off, pltpu.roll(x, off % n, 0), out)
    return out


def _prefix_rows(g, row):
    x = g
    d = 1
    while d < g.shape[0]:
        x = x + jnp.where(row >= d, pltpu.roll(x, d, 0), 0.0)
        d *= 2
    return x


def _suffix_rows(g, row):
    n = g.shape[0]
    x = g
    d = 1
    while d < n:
        x = x + jnp.where(row < n - d, pltpu.roll(x, n - d, 0), 0.0)
        d *= 2
    return x


def _pdim_intra(q, kf, kb, bf, cb, row, ii, jj):
    c = q.shape[0]
    att = jnp.where(ii == jj, _dot_nt(q.astype(BF16), (kf + kb).astype(BF16)), 0.0)
    s = 1
    while s < c:
        grp = 2 * s
        pos = row & (grp - 1)
        lo, hi = pos < s, pos >= s
        mf = _group_bcast(bf, row, grp, s - 1)
        mb = _group_bcast(cb, row, grp, s)
        qf = jnp.where(hi, q * jnp.exp(jnp.minimum(bf - mf, 0.0)), 0.0)
        kf_ = jnp.where(lo, kf * jnp.exp(jnp.minimum(mf - bf, 0.0)), 0.0)
        qb = jnp.where(lo, q * jnp.exp(jnp.minimum(cb - mb, 0.0)), 0.0)
        kb_ = jnp.where(hi, kb * jnp.exp(jnp.minimum(mb - cb, 0.0)), 0.0)
        qq = jnp.concatenate([qf, qb], axis=1).astype(BF16)
        kk = jnp.concatenate([kf_, kb_], axis=1).astype(BF16)
        att = att + jnp.where((ii ^ jj) < grp, _dot_nt(qq, kk), 0.0)
        s = grp
    return att


def _pdim_kernel(*refs, seq, mode, layer, has_init, want_state):
    it = iter(refs)
    if mode == "gla":
        q_ref, k_ref, v_ref, gf_ref, gb_ref, r_ref, nw_ref = (next(it) for _ in range(7))
    else:
        q_ref, v_ref, gf_ref, gb_ref, lbp_ref = (next(it) for _ in range(5))
    s0_ref = sfin_ref = None
    if has_init:
        s0_ref = next(it)
    next(it)
    y_ref = next(it)
    if want_state:
        sfin_ref = next(it)
    o_scr, sf_scr, sb_scr = (next(it) for _ in range(3))

    c = PDIM_CHUNK
    nchunks = seq // c
    row = lax.broadcasted_iota(jnp.int32, (c, 1), 0)
    ii = lax.broadcasted_iota(jnp.int32, (c, c), 0)
    jj = lax.broadcasted_iota(jnp.int32, (c, c), 1)

    if mode == "hgrn":
        p = lbp_ref[...]
        p = jnp.exp(p - jnp.max(p, axis=0, keepdims=True))
        p = p / jnp.sum(p, axis=0, keepdims=True)
        lb = jnp.zeros((1, p.shape[1]), F32)
        for r in range(1, layer + 1):
            lb = lb + p[r:r + 1, :]

    def load(n, need_f, need_b):
        rows = pl.ds(pl.multiple_of(n * c, c), c)
        v = v_ref[0, rows, :]
        if mode == "gla":
            q = q_ref[0, rows, :] * (GLA_DK ** -0.5)
            k = k_ref[0, rows, :]
            return q, k, k, (gf_ref[0, rows, :] if need_f else None), (gb_ref[0, rows, :] if need_b else None), v
        q = _silu(q_ref[0, rows, :])
        ff = lb + (1.0 - lb) * _sigmoid(gf_ref[0, rows, :]) if need_f else None
        fb = lb + (1.0 - lb) * _sigmoid(gb_ref[0, rows, :]) if need_b else None
        return (q, (1.0 - ff if need_f else None), (1.0 - fb if need_b else None),
                (jnp.log(ff) if need_f else None), (jnp.log(fb) if need_b else None), v)

    if has_init:
        sf_scr[...] = s0_ref[0, 0, 0].T
        sb_scr[...] = s0_ref[0, 1, 0].T
    else:
        sf_scr[...] = jnp.zeros(sf_scr.shape, F32)
        sb_scr[...] = jnp.zeros(sb_scr.shape, F32)

    def fwd(n, carry):
        q, kf, kb, gf, gb, v = load(n, True, True)
        bf = _prefix_rows(gf, row)
        cb = _suffix_rows(gb, row)
        vb = v.astype(BF16)
        att = _pdim_intra(q, kf, kb, bf, cb, row, ii, jj)
        o = _dot(att.astype(BF16), vb)
        o = o + _dot_nt((q * jnp.exp(bf)).astype(BF16), sf_scr[...].astype(BF16))
        o_scr[pl.ds(pl.multiple_of(n * c, c), c), :] = o
        b_end = bf[c - 1:c, :]
        sf_scr[...] = (sf_scr[...] * jnp.exp(b_end)
                       + _dot_tn(vb, (kf * jnp.exp(b_end - bf)).astype(BF16)))
        return carry

    lax.fori_loop(0, nchunks, fwd, 0)

    def bwd(t, carry):
        n = nchunks - 1 - t
        q, _, kb, _, gb, v = load(n, False, True)
        cb = _suffix_rows(gb, row)
        rows = pl.ds(pl.multiple_of(n * c, c), c)
        o_scr[rows, :] += _dot_nt((q * jnp.exp(cb)).astype(BF16), sb_scr[...].astype(BF16))
        c_end = cb[0:1, :]
        sb_scr[...] = (sb_scr[...] * jnp.exp(c_end)
                       + _dot_tn(v.astype(BF16), (kb * jnp.exp(c_end - cb)).astype(BF16)))
        return carry

    lax.fori_loop(0, nchunks, bwd, 0)

    if want_state:
        sfin_ref[0, 0, 0] = sf_scr[...].T
        sfin_ref[0, 1, 0] = sb_scr[...].T

    ec = min(seq, SCALAR_CHUNK)
    for n in range(seq // ec):
        rows = slice(n * ec, (n + 1) * ec)
        o = o_scr[rows, :]
        if mode == "gla":
            o = o * lax.rsqrt(jnp.mean(o * o, axis=-1, keepdims=True) + EPS)
            o = o * nw_ref[...] * _silu(r_ref[0, rows, :])
        y_ref[0, rows, :] = o.astype(y_ref.dtype)


def _pdim_scan(mode, proj, gates, extra, *, b, boff, layer=0, s0=None, want_state, y_init=None):
    _, t, _ = proj.shape
    if mode == "gla":
        hh, dk, dv = GLA_HEADS, GLA_DK, GLA_DV
        in_specs = [pl.BlockSpec((1, t, dk), lambda i, h: (boff + i, 0, h)),
                    pl.BlockSpec((1, t, dk), lambda i, h: (boff + i, 0, hh + h)),
                    pl.BlockSpec((1, t, dv), lambda i, h: (boff + i, 0, 2 * hh * dk // dv + h)),
                    pl.BlockSpec((1, t, dk), lambda i, h: (boff + i, 0, h)),
                    pl.BlockSpec((1, t, dk), lambda i, h: (boff + i, 0, hh + h)),
                    pl.BlockSpec((1, t, dv), lambda i, h: (boff + i, 0, (2 * hh * dk + hh * dv) // dv + h)),
                    pl.BlockSpec((1, dv), lambda i, h: (0, 0))]
        args = [proj, proj, proj, gates, gates, proj, extra.reshape(1, dv)]
        out_dtype = BF16
    else:
        hh, dk, dv = HG_HEADS, HG_DF, HG_DI
        in_specs = [pl.BlockSpec((1, t, dk), lambda i, h: (boff + i, 0, h)),
                    pl.BlockSpec((1, t, dv), lambda i, h: (boff + i, 0, hh + h)),
                    pl.BlockSpec((1, t, dk), lambda i, h: (boff + i, 0, 3 * hh + h)),
                    pl.BlockSpec((1, t, dk), lambda i, h: (boff + i, 0, 4 * hh + h)),
                    pl.BlockSpec((DEPTH, dk), lambda i, h: (0, h))]
        args = [proj, proj, proj, proj, extra]
        out_dtype = F32
    st_spec = pl.BlockSpec((1, 2, 1, dk, dv), lambda i, h: (i, 0, h, 0, 0))
    if s0 is not None:
        in_specs.append(st_spec)
        args.append(s0)
    y_init = _y_buffer(y_init, proj, hh * dv, out_dtype)
    in_specs.append(pl.BlockSpec(memory_space=pl.ANY))
    args.append(y_init)
    out_specs = [pl.BlockSpec((1, t, dv), lambda i, h: (boff + i, 0, h))]
    out_shape = [jax.ShapeDtypeStruct(y_init.shape, y_init.dtype)]
    if want_state:
        out_specs.append(st_spec)
        out_shape.append(jax.ShapeDtypeStruct((b, 2, hh, dk, dv), F32))
    res = pl.pallas_call(
        functools.partial(_pdim_kernel, seq=t, mode=mode, layer=layer, has_init=s0 is not None,
                          want_state=want_state),
        grid=(b, hh),
        in_specs=in_specs,
        out_specs=out_specs,
        out_shape=out_shape,
        scratch_shapes=[pltpu.VMEM((t, dv), F32), pltpu.VMEM((dv, dk), F32), pltpu.VMEM((dv, dk), F32)],
        input_output_aliases={len(args) - 1: 0},
        compiler_params=_cparams(2),
        name=mode + "_scan",
    )(*args)
    return res if want_state else (res[0], None)


ROW_TILE = 1024
GATHER_ROWS = 512
N_MOD_ROWS = 16


def _full(n, v):
    return jnp.full((n,), v, jnp.int32)


def _both_groups(scan, proj, tok, seqs, **lat):
    (bc, tc), (bl, tl) = seqs
    rows, n = proj.shape
    yc = scan(proj.reshape(rows // tc, tc, n), b=bc, boff=0, want_state=True)
    yl = scan(proj.reshape(rows // tl, tl, n), b=bl, boff=tok.n_ctx // tl, want_state=False, y_init=yc[0], **lat)
    return yl[0].reshape(rows, -1), yc[1:]


def kernel(x_prompt, x_sample, state_gla, state_mlstm_c, state_mlstm_n, state_mlstm_m, state_ret, state_hgrn, c, c_ctx, ada_w, ada_b, ln_g, ln_b, gla_w_in, gla_w_g1, gla_w_g2, gla_b_g, gla_norm_w, gla_w_out, ml_w_in, ml_w_gate, ml_b_gate, ml_norm_w, ml_w_out, ret_w_in, ret_decay, ret_norm_w, ret_w_out, hg_w_in, hg_w_f, hg_b_f, hg_lb, hg_norm_w, hg_w_out, ffn_w_gate, ffn_w_up, ffn_w_down, moe_router, moe_w_gate, moe_w_up, moe_w_down):
    bc, tc, d = x_prompt.shape
    bl, tl, _ = x_sample.shape
    n_ctx, n_lat = bc * tc, bl * tl
    m = n_ctx + n_lat
    tok = _Tokens(n_ctx, tl)
    tm = ROW_TILE
    assert d == D_MODEL and n_ctx % tm == 0 and tl % tm == 0 and 1 + bl <= N_MOD_ROWS and ada_w.shape[0] == DEPTH
    seqs = ((bc, tc), (bl, tl))
    x = jnp.concatenate([x_prompt.reshape(n_ctx, d), x_sample.reshape(n_lat, d)], axis=0)

    cvec = jnp.concatenate([c_ctx[None, :], c, jnp.zeros((N_MOD_ROWS - 1 - bl, d), F32)], axis=0)
    mods = [_linear(cvec, ada_w, widx=i, bias=ada_b[i], pro="silu", tm=N_MOD_ROWS).reshape(N_MOD_ROWS, 6, 1, d)
            for i in range(DEPTH)]
    n_tiles = m // tm

    def mix_out(y, w, i, rows=tm, hgrn=None):
        return _out_proj(y, w, _full(m // rows, 0), tk=min(y.shape[1], 1024), tm=rows,
                         resln=(x, mods[i], 2, ln_g[i, 0], ln_b[i, 0]), tok=tok, hgrn=hgrn)

    i = 0
    w_cat = jnp.concatenate([gla_w_in[0], gla_w_g1[0, 0], gla_w_g1[0, 1],
                             jnp.zeros((d, LANES - 2 * GLA_RANK), F32)], axis=1)
    proj = _linear(x, w_cat, mod=mods[i], mod_cols=(0, 1), tok=tok, pro="mod", tm=tm)
    nin = gla_w_in.shape[-1]
    half = GLA_HEADS * GLA_DK
    w2 = jnp.zeros((LANES, 2 * half), F32)
    w2 = w2.at[:GLA_RANK, :half].set(gla_w_g2[0, 0]).at[GLA_RANK:2 * GLA_RANK, half:].set(gla_w_g2[0, 1])
    gates = _linear(proj[:, nin:], w2, bias=gla_b_g[0].reshape(2 * half), epi="logsig_tau", tm=tm)
    scan = lambda p, **kw: _pdim_scan("gla", p, gates.reshape(p.shape[0], p.shape[1], 2 * half), gla_norm_w[0], **kw)
    y, (new_gla,) = _both_groups(scan, proj, tok, seqs, s0=state_gla[:, 0])
    x = mix_out(y, gla_w_out, i)
    u = _ffn_up(x, ffn_w_gate, ffn_w_up, _full(n_tiles, 0), mod=mods[i], tok=tok, tm=tm, tf=D_FF // 2)
    x = _out_proj(u, ffn_w_down, _full(n_tiles, 0), tk=D_FF // 2, tm=tm,
                  resln=(x, mods[i], 5, ln_g[i, 1], ln_b[i, 1]), tok=tok)

    i = 1
    wg0, wg1 = ml_w_gate[0, 0], ml_w_gate[0, 1]
    hh = ML_HEADS
    g4 = jnp.stack([wg0[:, :hh], wg0[:, hh:], wg1[:, :hh], wg1[:, hh:]], axis=-1)
    g4 = jnp.pad(g4, ((0, 0), (0, 0), (0, LANES - 4))).reshape(d, hh * LANES)
    b0, b1 = ml_b_gate[0, 0], ml_b_gate[0, 1]
    b4 = jnp.pad(jnp.stack([b0[:hh], b0[hh:], b1[:hh], b1[hh:]], axis=-1), ((0, 0), (0, LANES - 4)))
    w_cat = jnp.concatenate([ml_w_in[0], g4], axis=1)
    bias = jnp.concatenate([jnp.zeros((ml_w_in.shape[-1],), F32), b4.reshape(hh * LANES)])
    proj = _linear(x, w_cat, bias=bias, mod=mods[i], mod_cols=(0, 1), tok=tok, pro="mod", tm=tm)
    init = (state_mlstm_c[:, 0], state_mlstm_n[:, 0][:, :, :, None, :],
            jnp.broadcast_to(state_mlstm_m[:, 0][:, :, :, None, None], (bl, 2, hh, 1, LANES)))
    scan = lambda p, **kw: _mlstm_scan(p, ml_norm_w[0], **kw)
    y, (new_c, new_n, new_m) = _both_groups(scan, proj, tok, seqs, init=init)
    x = mix_out(y, ml_w_out, i)
    x = _moe_layer(x, mods[i], 0, moe_router, moe_w_gate, moe_w_up, moe_w_down, ln_g[i, 1], ln_b[i, 1], tok, tm)

    i = 2
    proj = _linear(x, ret_w_in, mod=mods[i], mod_cols=(0, 1), tok=tok, pro="mod", tm=tm)
    quarter = RET_DK // 4
    inv = jnp.power(ROPE_BASE, -jnp.arange(quarter, dtype=F32) / quarter)
    rws = jnp.repeat(jnp.arange(tl // GRID_W, dtype=F32), GRID_W)
    cls = jnp.tile(jnp.arange(GRID_W, dtype=F32), tl // GRID_W)
    ang = jnp.concatenate([rws[:, None] * inv, cls[:, None] * inv], axis=-1)
    scan = lambda p, **kw: _ret_scan(p, ret_decay[0], ret_norm_w[0], **kw)
    y, (new_ret,) = _both_groups(scan, proj, tok, seqs, s0=state_ret[:, 0],
                                 rope_tabs=(jnp.cos(ang), jnp.sin(ang)))
    x = mix_out(y, ret_w_out, i)
    u = _ffn_up(x, ffn_w_gate, ffn_w_up, _full(n_tiles, 1), mod=mods[i], tok=tok, tm=tm, tf=D_FF // 2)
    x = _out_proj(u, ffn_w_down, _full(n_tiles, 1), tk=D_FF // 2, tm=tm,
                  resln=(x, mods[i], 5, ln_g[i, 1], ln_b[i, 1]), tok=tok)

    i = 3
    w_cat = jnp.concatenate([hg_w_in[0], hg_w_f[0, 0], hg_w_f[0, 1]], axis=1)
    bias = jnp.concatenate([jnp.zeros((hg_w_in.shape[-1],), F32), hg_b_f[0, 0], hg_b_f[0, 1]])
    proj = _linear(x, w_cat, bias=bias, mod=mods[i], mod_cols=(0, 1), tok=tok, pro="mod", tm=tm)
    scan = lambda p, **kw: _pdim_scan("hgrn", p, None, hg_lb, layer=i, **kw)
    y, (new_hg,) = _both_groups(scan, proj, tok, seqs, s0=state_hgrn[:, 0])
    x = mix_out(y, hg_w_out, i, rows=tm // 2, hgrn=(proj, 2, hg_norm_w[0]))
    x = _moe_layer(x, mods[i], 1, moe_router, moe_w_gate, moe_w_up, moe_w_down, ln_g[i, 1], ln_b[i, 1], tok, tm)

    y_prompt = x[:n_ctx].reshape(bc, tc, d)
    y_sample = x[n_ctx:].reshape(bl, tl, d)
    return (y_prompt, y_sample, new_gla[:, None], new_c[:, None], new_n[:, None, :, :, 0, :],
            new_m[:, None, :, :, 0, 0], new_ret[:, None], new_hg[:, None])


def _moe_layer(x, mod, j, router, w_gate, w_up, w_down, lg, lb, tok, tm):
    m, d = x.shape
    ne = N_EXPERTS
    wr = jnp.pad(router, ((0, 0), (0, 0), (0, LANES - ne)))
    route = _router(x, mod, wr, widx=j, tok=tok, tm=tm)
    eid = jnp.concatenate([route[:, 0], route[:, 1]]).astype(jnp.int32)
    onehot = (eid[:, None] == jnp.arange(ne, dtype=jnp.int32)[None, :]).astype(jnp.int32)
    csum = jnp.cumsum(onehot, axis=0)
    rank = jnp.sum(onehot * csum, axis=1) - 1
    counts = csum[-1]
    padded = ((counts + tm - 1) // tm) * tm
    ends = jnp.cumsum(padded)
    pos = (ends - padded)[eid] + rank
    m_pad = 2 * m + ne * tm
    starts = jnp.arange(m_pad // tm, dtype=jnp.int32) * tm
    tile_e = jnp.minimum(jnp.sum((starts[:, None] >= ends[None, :]).astype(jnp.int32), axis=1), ne - 1) + ne * j
    xs = _dispatch(x, mod, pos[:m], pos[m:], m_pad, tok=tok, rows=tm)
    f = w_gate.shape[-1]
    us = _ffn_up(xs, w_gate.reshape(-1, d, f), w_up.reshape(-1, d, f), tile_e, tm=tm, tf=f // 2)
    ys = _out_proj(us, w_down.reshape(-1, f, d), tile_e, tk=f // 2, tm=tm)
    return _moe_combine(ys, route, pos[:m], pos[m:], x, mod, lg, lb, tok=tok, rows=GATHER_ROWS)
```

```python
import functools

import jax
import jax.numpy as jnp
from jax import lax
from jax.experimental import pallas as pl
from jax.experimental.pallas import tpu as pltpu

F32 = jnp.float32
BF16 = jnp.bfloat16
HIGHEST = lax.Precision.HIGHEST

D_MODEL = 1024
DEPTH = 4
GRID_W = 64
ALPHA = (2.0 * DEPTH) ** 0.25
EPS = 1e-5
ROPE_BASE = 10000.0
GLA_HEADS, GLA_DK, GLA_DV, GLA_RANK, GLA_TAU = 4, 128, 256, 16, 16.0
ML_HEADS, ML_D = 4, 256
RET_HEADS, RET_DK, RET_DV = 4, 256, 512
HG_HEADS, HG_DF, HG_DI = 8, 128, 128
D_FF = 2816
N_EXPERTS = 8

V7X_VMEM_LIMIT_BYTES = 56 * 1024 * 1024
LANES = 128
SUBLANES = 8

SCALAR_CHUNK = 256
PDIM_CHUNK = 64


def _cparams(n_axes):
    return pltpu.CompilerParams(dimension_semantics=("arbitrary",) * n_axes,
                                vmem_limit_bytes=V7X_VMEM_LIMIT_BYTES)


def _sigmoid(x):
    return jax.nn.sigmoid(x)


def _silu(x):
    return x * jax.nn.sigmoid(x)


def _log_sigmoid(x):
    return jnp.minimum(x, 0.0) - jnp.log1p(jnp.exp(-jnp.abs(x)))


def _dot(a, b):
    return jnp.dot(a, b, preferred_element_type=F32)


def _dot_nt(a, b):
    return lax.dot_general(a, b, (((1,), (1,)), ((), ())), preferred_element_type=F32)


def _dot_tn(a, b):
    return lax.dot_general(a, b, (((0,), (0,)), ((), ())), preferred_element_type=F32)


def _largest_tile(n, cap):
    best = None
    for t in range(LANES, min(n, cap) + 1, LANES):
        if n % t == 0:
            best = t
    assert best is not None, (n, cap)
    return best


class _Tokens:
    def __init__(self, n_ctx, lat_seq):
        self.n_ctx, self.lat_seq = n_ctx, lat_seq

    def group(self, i, tm):
        r = i * tm
        return jnp.where(r < self.n_ctx, 0, 1 + (r - self.n_ctx) // self.lat_seq)


def _linear_kernel(*refs, pro, epi, has_bias):
    it = iter(refs)
    x_ref = next(it)
    sh_ref = sc_ref = b_ref = None
    if pro == "mod":
        sh_ref, sc_ref = next(it), next(it)
    w_ref = next(it)
    if has_bias:
        b_ref = next(it)
    o_ref = next(it)
    h_scr = next(it)

    @pl.when(pl.program_id(1) == 0)
    def _():
        x = x_ref[...].astype(F32)
        if pro == "mod":
            x = x * (1.0 + sc_ref[...]) + sh_ref[...]
        elif pro == "silu":
            x = _silu(x)
        h_scr[...] = x.astype(BF16)

    acc = _dot(h_scr[...], w_ref[...].astype(BF16))
    if has_bias:
        acc = acc + b_ref[...]
    if epi == "logsig_tau":
        acc = _log_sigmoid(acc) / GLA_TAU
    o_ref[...] = acc.astype(o_ref.dtype)


def _linear(x, w, *, widx=0, bias=None, mod=None, mod_cols=None, tok=None, pro=None, epi=None,
            tm, tn_cap=1536, out_dtype=F32):
    m, k = x.shape
    w3 = w if w.ndim == 3 else w.reshape(1, *w.shape)
    n = w3.shape[-1]
    tn = _largest_tile(n, tn_cap)
    assert m % tm == 0
    in_specs = [pl.BlockSpec((tm, k), lambda i, j: (i, 0))]
    args = [x]
    if pro == "mod":
        for c in mod_cols:
            in_specs.append(pl.BlockSpec((None, None, 1, k),
                                         lambda i, j, c=c: (tok.group(i, tm), c, 0, 0)))
            args.append(mod)
    in_specs.append(pl.BlockSpec((None, k, tn), lambda i, j: (widx, 0, j)))
    args.append(w3)
    if bias is not None:
        in_specs.append(pl.BlockSpec((1, tn), lambda i, j: (0, j)))
        args.append(bias.reshape(1, n).astype(F32))
    return pl.pallas_call(
        functools.partial(_linear_kernel, pro=pro, epi=epi, has_bias=bias is not None),
        grid=(m // tm, n // tn),
        in_specs=in_specs,
        out_specs=pl.BlockSpec((tm, tn), lambda i, j: (i, j)),
        out_shape=jax.ShapeDtypeStruct((m, n), out_dtype),
        scratch_shapes=[pltpu.VMEM((tm, k), BF16)],
        compiler_params=_cparams(2),
        name="linear",
    )(*args)


FFN_COL_CHUNK = 256


def _ffn_up_kernel(e_ref, *refs, pro):
    it = iter(refs)
    x_ref = next(it)
    sh_ref = sc_ref = None
    if pro == "mod":
        sh_ref, sc_ref = next(it), next(it)
    wg_ref, wu_ref, o_ref, h_scr = next(it), next(it), next(it), next(it)

    @pl.when(pl.program_id(1) == 0)
    def _():
        x = x_ref[...].astype(F32)
        if pro == "mod":
            x = x * (1.0 + sc_ref[...]) + sh_ref[...]
        h_scr[...] = x.astype(BF16)

    h = h_scr[...]
    tf = o_ref.shape[-1]
    for c0 in range(0, tf, FFN_COL_CHUNK):
        c1 = min(c0 + FFN_COL_CHUNK, tf)
        g = _dot(h, wg_ref[:, c0:c1].astype(BF16))
        u = _dot(h, wu_ref[:, c0:c1].astype(BF16))
        o_ref[:, c0:c1] = (_silu(g) * u).astype(o_ref.dtype)


def _ffn_up(x, wg, wu, tile_expert, *, mod=None, tok=None, tm, tf):
    m, k = x.shape
    f = wg.shape[-1]
    pro = "mod" if mod is not None else None
    in_specs = [pl.BlockSpec((tm, k), lambda i, j, e: (i, 0))]
    args = [x]
    if pro == "mod":
        for c in (3, 4):
            in_specs.append(pl.BlockSpec((None, None, 1, k),
                                         lambda i, j, e, c=c: (tok.group(i, tm), c, 0, 0)))
            args.append(mod)
    wspec = pl.BlockSpec((None, k, tf), lambda i, j, e: (e[i], 0, j))
    in_specs += [wspec, wspec]
    args += [wg, wu]
    return pl.pallas_call(
        functools.partial(_ffn_up_kernel, pro=pro),
        grid_spec=pltpu.PrefetchScalarGridSpec(
            num_scalar_prefetch=1,
            grid=(m // tm, f // tf),
            in_specs=in_specs,
            out_specs=pl.BlockSpec((tm, tf), lambda i, j, e: (i, j)),
            scratch_shapes=[pltpu.VMEM((tm, k), BF16)],
        ),
        out_shape=jax.ShapeDtypeStruct((m, f), BF16),
        compiler_params=_cparams(2),
        name="ffn_up",
    )(tile_expert, *args)


def _layer_norm_rows(x, g, b):
    x = x - jnp.mean(x, axis=-1, keepdims=True)
    x = x * lax.rsqrt(jnp.mean(x * x, axis=-1, keepdims=True) + EPS)
    return x * g + b


def _out_kernel(e_ref, *refs, pro, epi, nk):
    it = iter(refs)
    y_ref = next(it)
    gin_ref = nw_ref = x_ref = gate_ref = lg_ref = lb_ref = None
    if pro == "hgrn":
        gin_ref, nw_ref = next(it), next(it)
    w_ref = next(it)
    if epi == "resln":
        x_ref, gate_ref, lg_ref, lb_ref = next(it), next(it), next(it), next(it)
    o_ref = next(it)
    kk = pl.program_id(1)

    y = y_ref[...]
    if pro == "hgrn":
        y = y.astype(F32)
        y = y * lax.rsqrt(jnp.mean(y * y, axis=-1, keepdims=True) + EPS)
        y = y * nw_ref[...] * _silu(gin_ref[...])
    part = _dot(y.astype(BF16), w_ref[...].astype(BF16))

    if nk > 1:
        @pl.when(kk == 0)
        def _():
            o_ref[...] = part

        @pl.when(kk > 0)
        def _():
            o_ref[...] += part

    def finish():
        acc = part if nk == 1 else o_ref[...]
        if epi == "resln":
            acc = _layer_norm_rows(ALPHA * x_ref[...] + gate_ref[...] * acc, lg_ref[...], lb_ref[...])
        o_ref[...] = acc

    if nk == 1:
        finish()
    elif epi == "resln":
        pl.when(kk == nk - 1)(finish)


def _out_proj(y, w, tile_expert, *, tk, tm, resln=None, hgrn=None, tok=None):
    m, k = y.shape
    d = w.shape[-1]
    nk = k // tk
    assert k % tk == 0 and (hgrn is None or nk == 1)
    in_specs = [pl.BlockSpec((tm, tk), lambda i, kk, e: (i, kk))]
    args = [y]
    if hgrn is not None:
        proj, gblk, nw = hgrn
        in_specs += [pl.BlockSpec((tm, k), lambda i, kk, e: (i, gblk)),
                     pl.BlockSpec((1, k), lambda i, kk, e: (0, 0))]
        args += [proj, nw.reshape(1, k)]
    in_specs.append(pl.BlockSpec((None, tk, d), lambda i, kk, e: (e[i], kk, 0)))
    args.append(w)
    if resln is not None:
        x, mod, gcol, lg, lb = resln
        in_specs += [pl.BlockSpec((tm, d), lambda i, kk, e: (i, 0)),
                     pl.BlockSpec((None, None, 1, d), lambda i, kk, e: (tok.group(i, tm), gcol, 0, 0)),
                     pl.BlockSpec((1, d), lambda i, kk, e: (0, 0)),
                     pl.BlockSpec((1, d), lambda i, kk, e: (0, 0))]
        args += [x, mod, lg.reshape(1, d), lb.reshape(1, d)]
    return pl.pallas_call(
        functools.partial(_out_kernel, pro="hgrn" if hgrn is not None else None,
                          epi="resln" if resln is not None else None, nk=nk),
        grid_spec=pltpu.PrefetchScalarGridSpec(
            num_scalar_prefetch=1,
            grid=(m // tm, nk),
            in_specs=in_specs,
            out_specs=pl.BlockSpec((tm, d), lambda i, kk, e: (i, 0)),
        ),
        out_shape=jax.ShapeDtypeStruct((m, d), F32),
        compiler_params=_cparams(2),
        name="out_proj",
    )(tile_expert, *args)


def _router_kernel(x_ref, sh_ref, sc_ref, wr_ref, r_ref):
    h = x_ref[...] * (1.0 + sc_ref[...]) + sh_ref[...]
    logits = _dot(h.astype(BF16), wr_ref[...].astype(BF16))
    lane = lax.broadcasted_iota(jnp.int32, logits.shape, 1)
    neg = jnp.float32(-jnp.inf)
    logits = jnp.where(lane < N_EXPERTS, logits, neg)
    m1 = jnp.max(logits, axis=-1, keepdims=True)
    i1 = jnp.min(jnp.where(logits == m1, lane, LANES), axis=-1, keepdims=True)
    rest = jnp.where(lane == i1, neg, logits)
    m2 = jnp.max(rest, axis=-1, keepdims=True)
    i2 = jnp.min(jnp.where(rest == m2, lane, LANES), axis=-1, keepdims=True)
    e2 = jnp.exp(m2 - m1)
    w1 = 1.0 / (1.0 + e2)
    w2 = e2 / (1.0 + e2)
    out = jnp.where(lane == 0, i1.astype(F32), 0.0)
    out = jnp.where(lane == 1, i2.astype(F32), out)
    out = jnp.where(lane == 2, w1, out)
    out = jnp.where(lane == 3, w2, out)
    r_ref[...] = out


def _router(x, mod, wr_pad, *, widx, tok, tm):
    m, k = x.shape
    return pl.pallas_call(
        _router_kernel,
        grid=(m // tm,),
        in_specs=[pl.BlockSpec((tm, k), lambda i: (i, 0)),
                  pl.BlockSpec((None, None, 1, k), lambda i: (tok.group(i, tm), 3, 0, 0)),
                  pl.BlockSpec((None, None, 1, k), lambda i: (tok.group(i, tm), 4, 0, 0)),
                  pl.BlockSpec((None, k, LANES), lambda i: (widx, 0, 0))],
        out_specs=pl.BlockSpec((tm, LANES), lambda i: (i, 0)),
        out_shape=jax.ShapeDtypeStruct((m, LANES), F32),
        compiler_params=_cparams(1),
        name="router",
    )(x, mod, mod, wr_pad)


DMA_ISSUE_UNROLL = 8


def _row_copy(src_ref, dst_ref, sem, src_row, dst_row):
    return pltpu.make_async_copy(src_ref.at[pl.ds(src_row, 1), :], dst_ref.at[pl.ds(dst_row, 1), :], sem)


def _dispatch_kernel(p1_ref, p2_ref, x_ref, sh_ref, sc_ref, init_ref, xs_ref, h_scr, sem, *, rows):
    del init_ref
    base = pl.program_id(0) * rows
    h_scr[...] = x_ref[...] * (1.0 + sc_ref[...]) + sh_ref[...]

    def start(r, c):
        _row_copy(h_scr, xs_ref, sem, r, p1_ref[base + r]).start()
        _row_copy(h_scr, xs_ref, sem, r, p2_ref[base + r]).start()
        return c

    lax.fori_loop(0, rows, start, 0, unroll=DMA_ISSUE_UNROLL)
    pltpu.make_async_copy(xs_ref.at[pl.ds(0, 2 * rows), :], xs_ref.at[pl.ds(0, 2 * rows), :], sem).wait()


def _dispatch(x, mod, p1, p2, n_out, *, tok, rows):
    m, d = x.shape
    return pl.pallas_call(
        functools.partial(_dispatch_kernel, rows=rows),
        grid_spec=pltpu.PrefetchScalarGridSpec(
            num_scalar_prefetch=2,
            grid=(m // rows,),
            in_specs=[pl.BlockSpec((rows, d), lambda i, p1, p2: (i, 0)),
                      pl.BlockSpec((None, None, 1, d), lambda i, p1, p2: (tok.group(i, rows), 3, 0, 0)),
                      pl.BlockSpec((None, None, 1, d), lambda i, p1, p2: (tok.group(i, rows), 4, 0, 0)),
                      pl.BlockSpec(memory_space=pl.ANY)],
            out_specs=pl.BlockSpec(memory_space=pl.ANY),
            scratch_shapes=[pltpu.VMEM((rows, d), F32), pltpu.SemaphoreType.DMA(())],
        ),
        out_shape=jax.ShapeDtypeStruct((n_out, d), F32),
        input_output_aliases={5: 0},
        compiler_params=_cparams(1),
        name="moe_dispatch",
    )(p1, p2, x, mod, mod, jnp.zeros((n_out, d), F32))


def _combine_kernel(p1_ref, p2_ref, ys_ref, r_ref, x_ref, gate_ref, lg_ref, lb_ref, o_ref,
                    a_scr, b_scr, sem, *, rows):
    base = pl.program_id(0) * rows

    def start(r, c):
        _row_copy(ys_ref, a_scr, sem.at[0], p1_ref[base + r], r).start()
        _row_copy(ys_ref, b_scr, sem.at[1], p2_ref[base + r], r).start()
        return c

    lax.fori_loop(0, rows, start, 0, unroll=DMA_ISSUE_UNROLL)
    pltpu.make_async_copy(ys_ref.at[pl.ds(0, rows), :], a_scr, sem.at[0]).wait()
    pltpu.make_async_copy(ys_ref.at[pl.ds(0, rows), :], b_scr, sem.at[1]).wait()
    rt = r_ref[...]
    f = rt[:, 2:3] * a_scr[...] + rt[:, 3:4] * b_scr[...]
    o_ref[...] = _layer_norm_rows(ALPHA * x_ref[...] + gate_ref[...] * f, lg_ref[...], lb_ref[...])


def _moe_combine(ys, route, p1, p2, x, mod, lg, lb, *, tok, rows):
    m, d = x.shape
    return pl.pallas_call(
        functools.partial(_combine_kernel, rows=rows),
        grid_spec=pltpu.PrefetchScalarGridSpec(
            num_scalar_prefetch=2,
            grid=(m // rows,),
            in_specs=[pl.BlockSpec(memory_space=pl.ANY),
                      pl.BlockSpec((rows, LANES), lambda i, p1, p2: (i, 0)),
                      pl.BlockSpec((rows, d), lambda i, p1, p2: (i, 0)),
                      pl.BlockSpec((None, None, 1, d), lambda i, p1, p2: (tok.group(i, rows), 5, 0, 0)),
                      pl.BlockSpec((1, d), lambda i, p1, p2: (0, 0)),
                      pl.BlockSpec((1, d), lambda i, p1, p2: (0, 0))],
            out_specs=pl.BlockSpec((rows, d), lambda i, p1, p2: (i, 0)),
            scratch_shapes=[pltpu.VMEM((rows, d), F32), pltpu.VMEM((rows, d), F32),
                            pltpu.SemaphoreType.DMA((2,))],
        ),
        out_shape=jax.ShapeDtypeStruct((m, d), F32),
        compiler_params=_cparams(1),
        name="moe_combine",
    )(p1, p2, ys, route, x, mod, lg.reshape(1, d), lb.reshape(1, d))


def _ret_kernel(dec_ref, *refs, seq, has_init, want_state, rope):
    it = iter(refs)
    q_ref, k_ref, v_ref, g_ref, nw_ref = (next(it) for _ in range(5))
    cos_ref = sin_ref = s0_ref = sfin_ref = None
    if rope:
        cos_ref, sin_ref = next(it), next(it)
    if has_init:
        s0_ref = next(it)
    next(it)
    y_ref = next(it)
    if want_state:
        sfin_ref = next(it)
    q_scr, k_scr, o_scr, sf_scr, sb_scr = (next(it) for _ in range(5))

    c = SCALAR_CHUNK
    nchunks = seq // c
    h = pl.program_id(1)
    lgf = _log_sigmoid(jnp.full((1, 1), dec_ref[h], F32))
    lgb = _log_sigmoid(jnp.full((1, 1), dec_ref[RET_HEADS + h], F32))

    ii = lax.broadcasted_iota(jnp.int32, (c, c), 0)
    jj = lax.broadcasted_iota(jnp.int32, (c, c), 1)
    dlt = (ii - jj).astype(F32)
    decay = (jnp.where(dlt >= 0, jnp.exp(lgf * jnp.maximum(dlt, 0.0)), 0.0)
             + jnp.where(dlt <= 0, jnp.exp(lgb * jnp.maximum(-dlt, 0.0)), 0.0))
    pos = lax.broadcasted_iota(jnp.int32, (c, 1), 0).astype(F32)
    q_in_f = jnp.exp(lgf * (pos + 1.0))
    q_in_b = jnp.exp(lgb * (c - pos))
    k_out_f = jnp.exp(lgf * (c - 1.0 - pos))
    k_out_b = jnp.exp(lgb * pos)
    tot_f = jnp.exp(lgf * c)
    tot_b = jnp.exp(lgb * c)

    half = RET_DK // 2
    for n in range(nchunks):
        rows = slice(n * c, (n + 1) * c)
        q = q_ref[0, rows, :]
        k = k_ref[0, rows, :] * (RET_DK ** -0.5)
        if rope:
            cs, sn = cos_ref[rows, :], sin_ref[rows, :]
            q = jnp.concatenate([q[:, :half] * cs - q[:, half:] * sn, q[:, :half] * sn + q[:, half:] * cs], axis=1)
            k = jnp.concatenate([k[:, :half] * cs - k[:, half:] * sn, k[:, :half] * sn + k[:, half:] * cs], axis=1)
        q_scr[rows, :] = q
        k_scr[rows, :] = k

    if has_init:
        sf_scr[...] = s0_ref[0, 0, 0]
        sb_scr[...] = s0_ref[0, 1, 0]

    for n in range(nchunks):
        rows = slice(n * c, (n + 1) * c)
        q, k = q_scr[rows, :], k_scr[rows, :]
        vb = v_ref[0, rows, :].astype(BF16)
        att = _dot_nt(q.astype(BF16), k.astype(BF16)) * decay
        o = _dot(att.astype(BF16), vb)
        live = has_init or n > 0
        if live:
            o = o + _dot((q * q_in_f).astype(BF16), sf_scr[...].astype(BF16))
        o_scr[rows, :] = o
        if n < nchunks - 1 or want_state:
            upd = _dot_tn((k * k_out_f).astype(BF16), vb)
            sf_scr[...] = tot_f * sf_scr[...] + upd if live else upd

    for n in reversed(range(nchunks)):
        rows = slice(n * c, (n + 1) * c)
        q, k = q_scr[rows, :], k_scr[rows, :]
        live = has_init or n < nchunks - 1
        if live:
            o_scr[rows, :] += _dot((q * q_in_b).astype(BF16), sb_scr[...].astype(BF16))
        if n > 0 or want_state:
            upd = _dot_tn((k * k_out_b).astype(BF16), v_ref[0, rows, :].astype(BF16))
            sb_scr[...] = tot_b * sb_scr[...] + upd if live else upd

    if want_state:
        sfin_ref[0, 0, 0] = sf_scr[...]
        sfin_ref[0, 1, 0] = sb_scr[...]

    for n in range(nchunks):
        rows = slice(n * c, (n + 1) * c)
        o = o_scr[rows, :]
        o = o - jnp.mean(o, axis=-1, keepdims=True)
        o = o * lax.rsqrt(jnp.mean(o * o, axis=-1, keepdims=True) + EPS)
        y_ref[0, rows, :] = (o * nw_ref[...] * _silu(g_ref[0, rows, :])).astype(y_ref.dtype)


def _y_buffer(y_init, proj, width, dtype):
    shape = (proj.shape[0], proj.shape[1], width)
    return jnp.zeros(shape, dtype) if y_init is None else y_init.reshape(shape)


def _ret_scan(proj, decay, norm_w, *, b, boff, s0=None, want_state, rope_tabs=None, y_init=None):
    _, t, _ = proj.shape
    hh, dk, dv = RET_HEADS, RET_DK, RET_DV
    koff, voff, goff = hh * dk // dk, 2 * hh * dk // dv, (2 * hh * dk + hh * dv) // dv
    in_specs = [pl.BlockSpec((1, t, dk), lambda i, h, d: (boff + i, 0, h)),
                pl.BlockSpec((1, t, dk), lambda i, h, d: (boff + i, 0, koff + h)),
                pl.BlockSpec((1, t, dv), lambda i, h, d: (boff + i, 0, voff + h)),
                pl.BlockSpec((1, t, dv), lambda i, h, d: (boff + i, 0, goff + h)),
                pl.BlockSpec((1, dv), lambda i, h, d: (0, h))]
    args = [proj, proj, proj, proj, norm_w.reshape(1, hh * dv)]
    if rope_tabs is not None:
        in_specs += [pl.BlockSpec((t, dk // 2), lambda i, h, d: (0, 0))] * 2
        args += list(rope_tabs)
    if s0 is not None:
        in_specs.append(pl.BlockSpec((1, 2, 1, dk, dv), lambda i, h, d: (i, 0, h, 0, 0)))
        args.append(s0)
    y_init = _y_buffer(y_init, proj, hh * dv, BF16)
    in_specs.append(pl.BlockSpec(memory_space=pl.ANY))
    args.append(y_init)
    out_specs = [pl.BlockSpec((1, t, dv), lambda i, h, d: (boff + i, 0, h))]
    out_shape = [jax.ShapeDtypeStruct(y_init.shape, y_init.dtype)]
    if want_state:
        out_specs.append(pl.BlockSpec((1, 2, 1, dk, dv), lambda i, h, d: (i, 0, h, 0, 0)))
        out_shape.append(jax.ShapeDtypeStruct((b, 2, hh, dk, dv), F32))
    res = pl.pallas_call(
        functools.partial(_ret_kernel, seq=t, has_init=s0 is not None, want_state=want_state,
                          rope=rope_tabs is not None),
        grid_spec=pltpu.PrefetchScalarGridSpec(
            num_scalar_prefetch=1,
            grid=(b, hh),
            in_specs=in_specs,
            out_specs=out_specs,
            scratch_shapes=[pltpu.VMEM((t, dk), F32), pltpu.VMEM((t, dk), F32), pltpu.VMEM((t, dv), F32),
                            pltpu.VMEM((dk, dv), F32), pltpu.VMEM((dk, dv), F32)],
        ),
        out_shape=out_shape,
        input_output_aliases={len(args): 0},
        compiler_params=_cparams(2),
        name="ret_scan",
    )(decay.reshape(2 * hh).astype(F32), *args)
    return res if want_state else (res[0], None)


def _mlstm_dir(q, qb, k, vb, s, cum_col, cum_row, ib_col, ib_row, mask, edge, c_ref, n_ref, m_ref,
               update_state):
    m_prev = m_ref[...]
    logw = jnp.where(mask, cum_col + (ib_row - cum_row), -jnp.inf)
    log_prev = cum_col + m_prev
    m_t = jnp.maximum(log_prev, jnp.max(logw, axis=1, keepdims=True))
    w_prev = jnp.exp(log_prev - m_t)
    qk = s * jnp.exp(logw - m_t)
    num = w_prev * _dot(qb, c_ref[...].astype(BF16)) + _dot(qk.astype(BF16), vb)
    den = w_prev * jnp.sum(q * n_ref[...], axis=1, keepdims=True) + jnp.sum(qk, axis=1, keepdims=True)
    out = num / jnp.maximum(jnp.abs(den), jnp.exp(-m_t))
    if update_state:
        m_new = m_t[edge:edge + 1, :]
        cum_edge = cum_col[edge:edge + 1, :]
        w_end = jnp.exp(cum_edge - cum_col + ib_col - m_new)
        dec = jnp.exp(cum_edge + m_prev - m_new)
        kw = k * w_end
        c_ref[...] = dec * c_ref[...] + _dot_tn(kw.astype(BF16), vb)
        n_ref[...] = dec * n_ref[...] + jnp.sum(kw, axis=0, keepdims=True)
        m_ref[...] = m_new
    return out


def _mlstm_kernel(*refs, seq, has_init, want_state):
    it = iter(refs)
    q_ref, k_ref, v_ref, og_ref, z_ref, nw_ref = (next(it) for _ in range(6))
    c0_ref = n0_ref = m0_ref = cf_ref = nf_ref = mf_ref = None
    if has_init:
        c0_ref, n0_ref, m0_ref = next(it), next(it), next(it)
    next(it)
    y_ref = next(it)
    if want_state:
        cf_ref, nf_ref, mf_ref = next(it), next(it), next(it)
    o_scr, c_scr, n_scr, m_scr = (next(it) for _ in range(4))

    c = SCALAR_CHUNK
    nchunks = seq // c
    ii = lax.broadcasted_iota(jnp.int32, (c, c), 0)
    jj = lax.broadcasted_iota(jnp.int32, (c, c), 1)
    lower = ii >= jj
    upper = ii <= jj
    tri_lo = lower.astype(F32)
    tri_up = upper.astype(F32)

    for d in range(2):
        if has_init:
            c_scr[d] = c0_ref[0, d, 0]
            n_scr[d] = n0_ref[0, d, 0]
            m_scr[d] = m0_ref[0, d, 0][:, :1]
        else:
            c_scr[d] = jnp.zeros(c_scr.shape[1:], F32)
            n_scr[d] = jnp.zeros(n_scr.shape[1:], F32)
            m_scr[d] = jnp.zeros(m_scr.shape[1:], F32)

    def chunk(n, d):
        rows = slice(n * c, (n + 1) * c)
        q = q_ref[0, rows, :]
        k = k_ref[0, rows, :] * (ML_D ** -0.5)
        qb, kb, vb = q.astype(BF16), k.astype(BF16), v_ref[0, rows, :].astype(BF16)
        s = _dot_nt(qb, kb)
        z = z_ref[0, rows, :]
        zt = z.T
        lf, lft = _log_sigmoid(z), _log_sigmoid(zt)
        last = (n == nchunks - 1) if d == 0 else (n == 0)
        upd = (not last) or want_state
        if d == 0:
            cum_col = jnp.dot(tri_lo, lf, precision=HIGHEST, preferred_element_type=F32)[:, 1:2]
            cum_row = jnp.dot(lft, tri_up, precision=HIGHEST, preferred_element_type=F32)[1:2, :]
            return _mlstm_dir(q, qb, k, vb, s, cum_col, cum_row, z[:, 0:1], zt[0:1, :], lower, c - 1,
                              c_scr.at[0], n_scr.at[0], m_scr.at[0], upd)
        cum_col = jnp.dot(tri_up, lf, precision=HIGHEST, preferred_element_type=F32)[:, 3:4]
        cum_row = jnp.dot(lft, tri_lo, precision=HIGHEST, preferred_element_type=F32)[3:4, :]
        return _mlstm_dir(q, qb, k, vb, s, cum_col, cum_row, z[:, 2:3], zt[2:3, :], upper, 0,
                          c_scr.at[1], n_scr.at[1], m_scr.at[1], upd)

    for n in range(nchunks):
        o_scr[n * c:(n + 1) * c, :] = chunk(n, 0)
    for n in reversed(range(nchunks)):
        o_scr[n * c:(n + 1) * c, :] += chunk(n, 1)

    if want_state:
        for d in range(2):
            cf_ref[0, d, 0] = c_scr[d]
            nf_ref[0, d, 0] = n_scr[d]
            mf_ref[0, d, 0] = jnp.broadcast_to(m_scr[d], (1, LANES))

    for n in range(nchunks):
        rows = slice(n * c, (n + 1) * c)
        o = o_scr[rows, :]
        o = o - jnp.mean(o, axis=-1, keepdims=True)
        o = o * lax.rsqrt(jnp.mean(o * o, axis=-1, keepdims=True) + EPS)
        y_ref[0, rows, :] = (o * nw_ref[...] * _sigmoid(og_ref[0, rows, :])).astype(y_ref.dtype)


def _mlstm_scan(proj, norm_w, *, b, boff, init=None, want_state, y_init=None):
    _, t, _ = proj.shape
    hh, d = ML_HEADS, ML_D
    zoff = 4 * hh * d // LANES
    in_specs = [pl.BlockSpec((1, t, d), lambda i, h: (boff + i, 0, h)),
                pl.BlockSpec((1, t, d), lambda i, h: (boff + i, 0, hh + h)),
                pl.BlockSpec((1, t, d), lambda i, h: (boff + i, 0, 2 * hh + h)),
                pl.BlockSpec((1, t, d), lambda i, h: (boff + i, 0, 3 * hh + h)),
                pl.BlockSpec((1, t, LANES), lambda i, h: (boff + i, 0, zoff + h)),
                pl.BlockSpec((1, d), lambda i, h: (0, h))]
    args = [proj] * 5 + [norm_w.reshape(1, hh * d)]
    st_specs = [pl.BlockSpec((1, 2, 1, d, d), lambda i, h: (i, 0, h, 0, 0)),
                pl.BlockSpec((1, 2, 1, 1, d), lambda i, h: (i, 0, h, 0, 0)),
                pl.BlockSpec((1, 2, 1, 1, LANES), lambda i, h: (i, 0, h, 0, 0))]
    if init is not None:
        in_specs += st_specs
        args += list(init)
    y_init = _y_buffer(y_init, proj, hh * d, BF16)
    in_specs.append(pl.BlockSpec(memory_space=pl.ANY))
    args.append(y_init)
    out_specs = [pl.BlockSpec((1, t, d), lambda i, h: (boff + i, 0, h))]
    out_shape = [jax.ShapeDtypeStruct(y_init.shape, y_init.dtype)]
    if want_state:
        out_specs += st_specs
        out_shape += [jax.ShapeDtypeStruct((b, 2, hh, d, d), F32),
                      jax.ShapeDtypeStruct((b, 2, hh, 1, d), F32),
                      jax.ShapeDtypeStruct((b, 2, hh, 1, LANES), F32)]
    res = pl.pallas_call(
        functools.partial(_mlstm_kernel, seq=t, has_init=init is not None, want_state=want_state),
        grid=(b, hh),
        in_specs=in_specs,
        out_specs=out_specs,
        out_shape=out_shape,
        scratch_shapes=[pltpu.VMEM((t, d), F32), pltpu.VMEM((2, d, d), F32), pltpu.VMEM((2, 1, d), F32),
                        pltpu.VMEM((2, 1, 1), F32)],
        input_output_aliases={len(args) - 1: 0},
        compiler_params=_cparams(2),
        name="mlstm_scan",
    )(*args)
    return res


def _group_bcast(x, row, group, r):
    n, d = x.shape
    if group >= SUBLANES:
        picked = x.reshape(n // group, group, d)[:, r:r + 1, :]
        return jnp.broadcast_to(picked, (n // group, group, d)).reshape(n, d)
    pos = row & (group - 1)
    out = x
    for off in range(-r, group - r):
        if off != 0:
            out = jnp.where(pos - r == off, pltpu.roll(x, off % n, 0), out)
    return out


def _prefix_rows(g, row):
    x = g
    d = 1
    while d < g.shape[0]:
        x = x + jnp.where(row >= d, pltpu.roll(x, d, 0), 0.0)
        d *= 2
    return x


def _suffix_rows(g, row):
    n = g.shape[0]
    x = g
    d = 1
    while d < n:
        x = x + jnp.where(row < n - d, pltpu.roll(x, n - d, 0), 0.0)
        d *= 2
    return x


def _pdim_intra(q, kf, kb, bf, cb, row, ii, jj):
    c = q.shape[0]
    att = jnp.where(ii == jj, _dot_nt(q.astype(BF16), (kf + kb).astype(BF16)), 0.0)
    xor = ii ^ jj
    below = ii > jj
    s = 1
    while s < c:
        grp = 2 * s
        mf = _group_bcast(bf, row, grp, s - 1)
        mb = _group_bcast(cb, row, grp, s)
        ef = jnp.exp(-jnp.abs(bf - mf))
        eb = jnp.exp(-jnp.abs(cb - mb))
        pf = _dot_nt((q * ef).astype(BF16), (kf * ef).astype(BF16))
        pb = _dot_nt((q * eb).astype(BF16), (kb * eb).astype(BF16))
        att = att + jnp.where((xor >= s) & (xor < grp), jnp.where(below, pf, pb), 0.0)
        s = grp
    return att


def _pdim_kernel(*refs, seq, mode, layer, has_init, want_state):
    it = iter(refs)
    if mode == "gla":
        q_ref, k_ref, v_ref, gf_ref, gb_ref, r_ref, nw_ref = (next(it) for _ in range(7))
    else:
        q_ref, v_ref, gf_ref, gb_ref, lbp_ref = (next(it) for _ in range(5))
    s0_ref = sfin_ref = None
    if has_init:
        s0_ref = next(it)
    next(it)
    y_ref = next(it)
    if want_state:
        sfin_ref = next(it)
    o_scr, sf_scr, sb_scr = (next(it) for _ in range(3))

    c = PDIM_CHUNK
    nchunks = seq // c
    row = lax.broadcasted_iota(jnp.int32, (c, 1), 0)
    ii = lax.broadcasted_iota(jnp.int32, (c, c), 0)
    jj = lax.broadcasted_iota(jnp.int32, (c, c), 1)

    if mode == "hgrn":
        p = lbp_ref[...]
        p = jnp.exp(p - jnp.max(p, axis=0, keepdims=True))
        p = p / jnp.sum(p, axis=0, keepdims=True)
        lb = jnp.zeros((1, p.shape[1]), F32)
        for r in range(1, layer + 1):
            lb = lb + p[r:r + 1, :]

    def load(n, need_f, need_b):
        rows = pl.ds(pl.multiple_of(n * c, c), c)
        v = v_ref[0, rows, :]
        if mode == "gla":
            q = q_ref[0, rows, :] * (GLA_DK ** -0.5)
            k = k_ref[0, rows, :]
            return q, k, k, (gf_ref[0, rows, :] if need_f else None), (gb_ref[0, rows, :] if need_b else None), v
        q = _silu(q_ref[0, rows, :])
        ff = lb + (1.0 - lb) * _sigmoid(gf_ref[0, rows, :]) if need_f else None
        fb = lb + (1.0 - lb) * _sigmoid(gb_ref[0, rows, :]) if need_b else None
        return (q, (1.0 - ff if need_f else None), (1.0 - fb if need_b else None),
                (jnp.log(ff) if need_f else None), (jnp.log(fb) if need_b else None), v)

    if has_init:
        sf_scr[...] = s0_ref[0, 0, 0].T
        sb_scr[...] = s0_ref[0, 1, 0].T
    else:
        sf_scr[...] = jnp.zeros(sf_scr.shape, F32)
        sb_scr[...] = jnp.zeros(sb_scr.shape, F32)

    def fwd(n, carry):
        q, kf, kb, gf, gb, v = load(n, True, True)
        bf = _prefix_rows(gf, row)
        cb = _suffix_rows(gb, row)
        vb = v.astype(BF16)
        att = _pdim_intra(q, kf, kb, bf, cb, row, ii, jj)
        o = _dot(att.astype(BF16), vb)
        o = o + _dot_nt((q * jnp.exp(bf)).astype(BF16), sf_scr[...].astype(BF16))
        o_scr[pl.ds(pl.multiple_of(n * c, c), c), :] = o
        b_end = bf[c - 1:c, :]
        sf_scr[...] = (sf_scr[...] * jnp.exp(b_end)
                       + _dot_tn(vb, (kf * jnp.exp(b_end - bf)).astype(BF16)))
        return carry

    lax.fori_loop(0, nchunks, fwd, 0)

    def bwd(t, carry):
        n = nchunks - 1 - t
        q, _, kb, _, gb, v = load(n, False, True)
        cb = _suffix_rows(gb, row)
        rows = pl.ds(pl.multiple_of(n * c, c), c)
        o_scr[rows, :] += _dot_nt((q * jnp.exp(cb)).astype(BF16), sb_scr[...].astype(BF16))
        c_end = cb[0:1, :]
        sb_scr[...] = (sb_scr[...] * jnp.exp(c_end)
                       + _dot_tn(v.astype(BF16), (kb * jnp.exp(c_end - cb)).astype(BF16)))
        return carry

    lax.fori_loop(0, nchunks, bwd, 0)

    if want_state:
        sfin_ref[0, 0, 0] = sf_scr[...].T
        sfin_ref[0, 1, 0] = sb_scr[...].T

    ec = min(seq, SCALAR_CHUNK)
    for n in range(seq // ec):
        rows = slice(n * ec, (n + 1) * ec)
        o = o_scr[rows, :]
        if mode == "gla":
            o = o * lax.rsqrt(jnp.mean(o * o, axis=-1, keepdims=True) + EPS)
            o = o * nw_ref[...] * _silu(r_ref[0, rows, :])
        y_ref[0, rows, :] = o.astype(y_ref.dtype)


def _pdim_scan(mode, proj, gates, extra, *, b, boff, layer=0, s0=None, want_state, y_init=None):
    _, t, _ = proj.shape
    if mode == "gla":
        hh, dk, dv = GLA_HEADS, GLA_DK, GLA_DV
        in_specs = [pl.BlockSpec((1, t, dk), lambda i, h: (boff + i, 0, h)),
                    pl.BlockSpec((1, t, dk), lambda i, h: (boff + i, 0, hh + h)),
                    pl.BlockSpec((1, t, dv), lambda i, h: (boff + i, 0, 2 * hh * dk // dv + h)),
                    pl.BlockSpec((1, t, dk), lambda i, h: (boff + i, 0, h)),
                    pl.BlockSpec((1, t, dk), lambda i, h: (boff + i, 0, hh + h)),
                    pl.BlockSpec((1, t, dv), lambda i, h: (boff + i, 0, (2 * hh * dk + hh * dv) // dv + h)),
                    pl.BlockSpec((1, dv), lambda i, h: (0, 0))]
        args = [proj, proj, proj, gates, gates, proj, extra.reshape(1, dv)]
        out_dtype = BF16
    else:
        hh, dk, dv = HG_HEADS, HG_DF, HG_DI
        in_specs = [pl.BlockSpec((1, t, dk), lambda i, h: (boff + i, 0, h)),
                    pl.BlockSpec((1, t, dv), lambda i, h: (boff + i, 0, hh + h)),
                    pl.BlockSpec((1, t, dk), lambda i, h: (boff + i, 0, 3 * hh + h)),
                    pl.BlockSpec((1, t, dk), lambda i, h: (boff + i, 0, 4 * hh + h)),
                    pl.BlockSpec((DEPTH, dk), lambda i, h: (0, h))]
        args = [proj, proj, proj, proj, extra]
        out_dtype = F32
    st_spec = pl.BlockSpec((1, 2, 1, dk, dv), lambda i, h: (i, 0, h, 0, 0))
    if s0 is not None:
        in_specs.append(st_spec)
        args.append(s0)
    y_init = _y_buffer(y_init, proj, hh * dv, out_dtype)
    in_specs.append(pl.BlockSpec(memory_space=pl.ANY))
    args.append(y_init)
    out_specs = [pl.BlockSpec((1, t, dv), lambda i, h: (boff + i, 0, h))]
    out_shape = [jax.ShapeDtypeStruct(y_init.shape, y_init.dtype)]
    if want_state:
        out_specs.append(st_spec)
        out_shape.append(jax.ShapeDtypeStruct((b, 2, hh, dk, dv), F32))
    res = pl.pallas_call(
        functools.partial(_pdim_kernel, seq=t, mode=mode, layer=layer, has_init=s0 is not None,
                          want_state=want_state),
        grid=(b, hh),
        in_specs=in_specs,
        out_specs=out_specs,
        out_shape=out_shape,
        scratch_shapes=[pltpu.VMEM((t, dv), F32), pltpu.VMEM((dv, dk), F32), pltpu.VMEM((dv, dk), F32)],
        input_output_aliases={len(args) - 1: 0},
        compiler_params=_cparams(2),
        name=mode + "_scan",
    )(*args)
    return res if want_state else (res[0], None)


ROW_TILE = 1024
GATHER_ROWS = 512
N_MOD_ROWS = 16


def _full(n, v):
    return jnp.full((n,), v, jnp.int32)


def _both_groups(scan, proj, tok, seqs, **lat):
    (bc, tc), (bl, tl) = seqs
    rows, n = proj.shape
    yc = scan(proj.reshape(rows // tc, tc, n), b=bc, boff=0, want_state=True)
    yl = scan(proj.reshape(rows // tl, tl, n), b=bl, boff=tok.n_ctx // tl, want_state=False, y_init=yc[0], **lat)
    return yl[0].reshape(rows, -1), yc[1:]


def kernel(x_prompt, x_sample, state_gla, state_mlstm_c, state_mlstm_n, state_mlstm_m, state_ret, state_hgrn, c, c_ctx, ada_w, ada_b, ln_g, ln_b, gla_w_in, gla_w_g1, gla_w_g2, gla_b_g, gla_norm_w, gla_w_out, ml_w_in, ml_w_gate, ml_b_gate, ml_norm_w, ml_w_out, ret_w_in, ret_decay, ret_norm_w, ret_w_out, hg_w_in, hg_w_f, hg_b_f, hg_lb, hg_norm_w, hg_w_out, ffn_w_gate, ffn_w_up, ffn_w_down, moe_router, moe_w_gate, moe_w_up, moe_w_down):
    bc, tc, d = x_prompt.shape
    bl, tl, _ = x_sample.shape
    n_ctx, n_lat = bc * tc, bl * tl
    m = n_ctx + n_lat
    tok = _Tokens(n_ctx, tl)
    tm = ROW_TILE
    assert d == D_MODEL and n_ctx % tm == 0 and tl % tm == 0 and 1 + bl <= N_MOD_ROWS and ada_w.shape[0] == DEPTH
    seqs = ((bc, tc), (bl, tl))
    x = jnp.concatenate([x_prompt.reshape(n_ctx, d), x_sample.reshape(n_lat, d)], axis=0)

    cvec = jnp.concatenate([c_ctx[None, :], c, jnp.zeros((N_MOD_ROWS - 1 - bl, d), F32)], axis=0)
    mods = [_linear(cvec, ada_w, widx=i, bias=ada_b[i], pro="silu", tm=N_MOD_ROWS).reshape(N_MOD_ROWS, 6, 1, d)
            for i in range(DEPTH)]
    n_tiles = m // tm

    def mix_out(y, w, i, rows=tm, hgrn=None):
        return _out_proj(y, w, _full(m // rows, 0), tk=min(y.shape[1], 1024), tm=rows,
                         resln=(x, mods[i], 2, ln_g[i, 0], ln_b[i, 0]), tok=tok, hgrn=hgrn)

    i = 0
    w_cat = jnp.concatenate([gla_w_in[0], gla_w_g1[0, 0], gla_w_g1[0, 1],
                             jnp.zeros((d, LANES - 2 * GLA_RANK), F32)], axis=1)
    proj = _linear(x, w_cat, mod=mods[i], mod_cols=(0, 1), tok=tok, pro="mod", tm=tm)
    nin = gla_w_in.shape[-1]
    half = GLA_HEADS * GLA_DK
    w2 = jnp.zeros((LANES, 2 * half), F32)
    w2 = w2.at[:GLA_RANK, :half].set(gla_w_g2[0, 0]).at[GLA_RANK:2 * GLA_RANK, half:].set(gla_w_g2[0, 1])
    gates = _linear(proj[:, nin:], w2, bias=gla_b_g[0].reshape(2 * half), epi="logsig_tau", tm=tm)
    scan = lambda p, **kw: _pdim_scan("gla", p, gates.reshape(p.shape[0], p.shape[1], 2 * half), gla_norm_w[0], **kw)
    y, (new_gla,) = _both_groups(scan, proj, tok, seqs, s0=state_gla[:, 0])
    x = mix_out(y, gla_w_out, i)
    u = _ffn_up(x, ffn_w_gate, ffn_w_up, _full(n_tiles, 0), mod=mods[i], tok=tok, tm=tm, tf=D_FF // 2)
    x = _out_proj(u, ffn_w_down, _full(n_tiles, 0), tk=D_FF // 2, tm=tm,
                  resln=(x, mods[i], 5, ln_g[i, 1], ln_b[i, 1]), tok=tok)

    i = 1
    wg0, wg1 = ml_w_gate[0, 0], ml_w_gate[0, 1]
    hh = ML_HEADS
    g4 = jnp.stack([wg0[:, :hh], wg0[:, hh:], wg1[:, :hh], wg1[:, hh:]], axis=-1)
    g4 = jnp.pad(g4, ((0, 0), (0, 0), (0, LANES - 4))).reshape(d, hh * LANES)
    b0, b1 = ml_b_gate[0, 0], ml_b_gate[0, 1]
    b4 = jnp.pad(jnp.stack([b0[:hh], b0[hh:], b1[:hh], b1[hh:]], axis=-1), ((0, 0), (0, LANES - 4)))
    w_cat = jnp.concatenate([ml_w_in[0], g4], axis=1)
    bias = jnp.concatenate([jnp.zeros((ml_w_in.shape[-1],), F32), b4.reshape(hh * LANES)])
    proj = _linear(x, w_cat, bias=bias, mod=mods[i], mod_cols=(0, 1), tok=tok, pro="mod", tm=tm)
    init = (state_mlstm_c[:, 0], state_mlstm_n[:, 0][:, :, :, None, :],
            jnp.broadcast_to(state_mlstm_m[:, 0][:, :, :, None, None], (bl, 2, hh, 1, LANES)))
    scan = lambda p, **kw: _mlstm_scan(p, ml_norm_w[0], **kw)
    y, (new_c, new_n, new_m) = _both_groups(scan, proj, tok, seqs, init=init)
    x = mix_out(y, ml_w_out, i)
    x = _moe_layer(x, mods[i], 0, moe_router, moe_w_gate, moe_w_up, moe_w_down, ln_g[i, 1], ln_b[i, 1], tok, tm)

    i = 2
    proj = _linear(x, ret_w_in, mod=mods[i], mod_cols=(0, 1), tok=tok, pro="mod", tm=tm)
    quarter = RET_DK // 4
    inv = jnp.power(ROPE_BASE, -jnp.arange(quarter, dtype=F32) / quarter)
    rws = jnp.repeat(jnp.arange(tl // GRID_W, dtype=F32), GRID_W)
    cls = jnp.tile(jnp.arange(GRID_W, dtype=F32), tl // GRID_W)
    ang = jnp.concatenate([rws[:, None] * inv, cls[:, None] * inv], axis=-1)
    scan = lambda p, **kw: _ret_scan(p, ret_decay[0], ret_norm_w[0], **kw)
    y, (new_ret,) = _both_groups(scan, proj, tok, seqs, s0=state_ret[:, 0],
                                 rope_tabs=(jnp.cos(ang), jnp.sin(ang)))
    x = mix_out(y, ret_w_out, i)
    u = _ffn_up(x, ffn_w_gate, ffn_w_up, _full(n_tiles, 1), mod=mods[i], tok=tok, tm=tm, tf=D_FF // 2)
    x = _out_proj(u, ffn_w_down, _full(n_tiles, 1), tk=D_FF // 2, tm=tm,
                  resln=(x, mods[i], 5, ln_g[i, 1], ln_b[i, 1]), tok=tok)

    i = 3
    w_cat = jnp.concatenate([hg_w_in[0], hg_w_f[0, 0], hg_w_f[0, 1]], axis=1)
    bias = jnp.concatenate([jnp.zeros((hg_w_in.shape[-1],), F32), hg_b_f[0, 0], hg_b_f[0, 1]])
    proj = _linear(x, w_cat, bias=bias, mod=mods[i], mod_cols=(0, 1), tok=tok, pro="mod", tm=tm)
    scan = lambda p, **kw: _pdim_scan("hgrn", p, None, hg_lb, layer=i, **kw)
    y, (new_hg,) = _both_groups(scan, proj, tok, seqs, s0=state_hgrn[:, 0])
    x = mix_out(y, hg_w_out, i, rows=tm // 2, hgrn=(proj, 2, hg_norm_w[0]))
    x = _moe_layer(x, mods[i], 1, moe_router, moe_w_gate, moe_w_up, moe_w_down, ln_g[i, 1], ln_b[i, 1], tok, tm)

    y_prompt = x[:n_ctx].reshape(bc, tc, d)
    y_sample = x[n_ctx:].reshape(bl, tl, d)
    return (y_prompt, y_sample, new_gla[:, None], new_c[:, None], new_n[:, None, :, :, 0, :],
            new_m[:, None, :, :, 0, 0], new_ret[:, None], new_hg[:, None])


def _moe_layer(x, mod, j, router, w_gate, w_up, w_down, lg, lb, tok, tm):
    m, d = x.shape
    ne = N_EXPERTS
    wr = jnp.pad(router, ((0, 0), (0, 0), (0, LANES - ne)))
    route = _router(x, mod, wr, widx=j, tok=tok, tm=tm)
    eid = jnp.concatenate([route[:, 0], route[:, 1]]).astype(jnp.int32)
    onehot = (eid[:, None] == jnp.arange(ne, dtype=jnp.int32)[None, :]).astype(jnp.int32)
    csum = jnp.cumsum(onehot, axis=0)
    rank = jnp.sum(onehot * csum, axis=1) - 1
    counts = csum[-1]
    padded = ((counts + tm - 1) // tm) * tm
    ends = jnp.cumsum(padded)
    pos = (ends - padded)[eid] + rank
    m_pad = 2 * m + ne * tm
    starts = jnp.arange(m_pad // tm, dtype=jnp.int32) * tm
    tile_e = jnp.minimum(jnp.sum((starts[:, None] >= ends[None, :]).astype(jnp.int32), axis=1), ne - 1) + ne * j
    xs = _dispatch(x, mod, pos[:m], pos[m:], m_pad, tok=tok, rows=tm)
    f = w_gate.shape[-1]
    us = _ffn_up(xs, w_gate.reshape(-1, d, f), w_up.reshape(-1, d, f), tile_e, tm=tm, tf=f // 2)
    ys = _out_proj(us, w_down.reshape(-1, f, d), tile_e, tk=f // 2, tm=tm)
    return _moe_combine(ys, route, pos[:m], pos[m:], x, mod, lg, lb, tok=tok, rows=GATHER_ROWS)
```
